```python
import math
import jax, jax.numpy as jnp
from jax import lax
import numpy as np

D_MODEL = 1024
BATCH = 8
SEQ = 4096
DEPTH = 2
DEC_BATCH = 8
DEC_SEQ = 8192
PAST_LEN = 128

EPS = 1e-6
CHUNK = 64
CONV_K = 5
D_FF = 4 * D_MODEL
GLA_HEADS = 4
GLA_DK = D_MODEL // 16
GLA_DV = D_MODEL // 8
GLA_RANK = 16
GLA_NORMALIZER = 16.0
GDN_HEADS = 4
GDN_DH = D_MODEL // 8
SSD_DINNER = D_MODEL
SSD_HEADDIM = 64
SSD_HEADS = SSD_DINNER // SSD_HEADDIM
SSD_GROUPS = 2
SSD_HPG = SSD_HEADS // SSD_GROUPS
SSD_DSTATE = 64

GLA_QK = GLA_HEADS * GLA_DK
GLA_V = GLA_HEADS * GLA_DV
GDN_W = GDN_HEADS * GDN_DH
SSD_CONV_CH = SSD_DINNER + 2 * SSD_GROUPS * SSD_DSTATE
D_MIX = GLA_V + GDN_W + SSD_DINNER
IN_SPLITS = (GLA_QK, GLA_QK, GLA_V, GLA_V, 2 * GLA_RANK,
             3 * GDN_W, GDN_W, 2 * GDN_HEADS, 2 * GDN_HEADS,
             SSD_DINNER, SSD_CONV_CH, 2 * SSD_HEADS)
D_IN_PROJ = 2 * GLA_QK + 2 * GLA_V + 2 * GLA_RANK + 4 * GDN_W + 4 * GDN_HEADS + SSD_DINNER + SSD_CONV_CH + 2 * SSD_HEADS

kernel_name = "hymba_style_bidir_gla_gdn_ssd_encoder"


def rmsnorm(x, w):
    xf = x.astype(jnp.float32)
    y = xf * lax.rsqrt(jnp.mean(xf * xf, axis=-1, keepdims=True) + EPS)
    return (y * w.astype(jnp.float32)).astype(x.dtype)


def l2norm(x):
    return x * lax.rsqrt(jnp.sum(x * x, axis=-1, keepdims=True) + EPS)


def flip(t):
    return jnp.flip(t, axis=1)


def dwconv(x, w):
    return lax.conv_general_dilated(
        x, w[:, None, :], window_strides=(1,),
        padding=((CONV_K // 2, CONV_K // 2),),
        dimension_numbers=("NWC", "WIO", "NWC"),
        feature_group_count=x.shape[-1])


def _chunk(t):
    return t.reshape(t.shape[0], t.shape[1] // CHUNK, CHUNK, *t.shape[2:])


def _unchunk(t):
    return t.reshape(t.shape[0], t.shape[1] * t.shape[2], *t.shape[3:])


def _decay_matrix(a):
    tril = jnp.tril(jnp.ones((CHUNK, CHUNK), dtype=bool))
    diff = a[..., :, None] - a[..., None, :]
    return jnp.exp(jnp.where(tril, diff, -jnp.inf))


def scan_states(decay, d_state):
    dec_t = jnp.moveaxis(decay, 1, 0)
    ds_t = jnp.moveaxis(d_state, 1, 0)

    def step(s, inp):
        a, d = inp
        return a * s + d, s

    _, states = lax.scan(step, jnp.zeros_like(ds_t[0]), (dec_t, ds_t))
    return jnp.moveaxis(states, 0, 1)


def gla_dir(q, k, v, g):
    q, k, v, g = (_chunk(t) for t in (q, k, v, g))
    G = jnp.cumsum(g, axis=2)
    g_tot = G[:, :, -1]
    q_in = q * jnp.exp(G)
    k_in = k * jnp.exp(-G)
    tril = jnp.tril(jnp.ones((CHUNK, CHUNK), dtype=bool))
    att = jnp.where(tril, jnp.einsum('bnihd,bnjhd->bnhij', q_in, k_in), 0.0)
    o = jnp.einsum('bnhij,bnjhv->bnihv', att, v)
    d_state = jnp.einsum('bnjhd,bnjhv->bnhdv', k * jnp.exp(g_tot[:, :, None] - G), v)
    states = scan_states(jnp.exp(g_tot)[..., None], d_state)
    o = o + jnp.einsum('bnihd,bnhdv->bnihv', q_in, states)
    return _unchunk(o)


def gla_mixer(q, k, v, gate, lr, up_w, up_b, norm_w):
    b, s = q.shape[:2]
    q = q.reshape(b, s, GLA_HEADS, GLA_DK) * (GLA_DK ** -0.5)
    k = k.reshape(b, s, GLA_HEADS, GLA_DK)
    v = v.reshape(b, s, GLA_HEADS, GLA_DV)
    lr = lr.reshape(b, s, 2, GLA_RANK)
    gk = jax.nn.log_sigmoid(jnp.einsum('bsdr,drk->bsdk', lr, up_w) + up_b) / GLA_NORMALIZER
    gk = gk.reshape(b, s, 2, GLA_HEADS, GLA_DK)
    o = gla_dir(q, k, v, gk[:, :, 0]) + flip(gla_dir(flip(q), flip(k), flip(v), flip(gk[:, :, 1])))
    o = rmsnorm(o, norm_w) * jax.nn.silu(gate.reshape(b, s, GLA_HEADS, GLA_DV))
    return o.reshape(b, s, GLA_V)


def gdn_dir(q, k, v, beta, g):
    dk = q.shape[-1]
    q, k, v, beta, g = (_chunk(t) for t in (q, k, v, beta, g))
    G = jnp.cumsum(g, axis=2)
    Gh = jnp.swapaxes(G, 2, 3)
    L = _decay_matrix(Gh)
    idx = jnp.arange(CHUNK)
    strict = idx[:, None] > idx[None, :]
    kk = jnp.einsum('bnihd,bnjhd->bnhij', k, k)
    bh = jnp.swapaxes(beta, 2, 3)
    tri = jnp.eye(CHUNK, dtype=kk.dtype) + jnp.where(strict, bh[..., :, None] * kk * L, 0.0)
    rhs = jnp.concatenate([k * (beta * jnp.exp(G))[..., None], v * beta[..., None]], axis=-1)
    rhs = jnp.swapaxes(rhs, 2, 3)
    sol = lax.linalg.triangular_solve(tri, rhs, left_side=True, lower=True, unit_diagonal=True)
    w_blk, u_blk = sol[..., :dk], sol[..., dk:]
    a_qk = jnp.einsum('bnihd,bnjhd->bnhij', q, k) * L
    q_dec = jnp.swapaxes(q * jnp.exp(G)[..., None], 2, 3)
    g_tot = Gh[..., -1]
    k_dec = jnp.swapaxes(k * jnp.exp(g_tot[:, :, None, :] - G)[..., None], 2, 3)

    def step(s, inp):
        wc, uc, qc, kc, ac, ec = inp
        u = uc - jnp.einsum('bhcd,bhdv->bhcv', wc, s)
        o = jnp.einsum('bhcd,bhdv->bhcv', qc, s) + jnp.einsum('bhij,bhjv->bhiv', ac, u)
        s = ec[..., None, None] * s + jnp.einsum('bhcd,bhcv->bhdv', kc, u)
        return s, o

    xs = tuple(jnp.moveaxis(t, 1, 0) for t in (w_blk, u_blk, q_dec, k_dec, a_qk, jnp.exp(g_tot)))
    s0 = jnp.zeros((q.shape[0], q.shape[3], dk, v.shape[-1]), dtype=q.dtype)
    _, o = lax.scan(step, s0, xs)
    o = jnp.swapaxes(jnp.moveaxis(o, 0, 1), 2, 3)
    return _unchunk(o)


def gdn_mixer(qkv, gate, beta_raw, a_raw, conv_w, A_log, dt_bias, norm_w):
    b, s = qkv.shape[:2]
    qkv = jax.nn.silu(dwconv(qkv, conv_w))
    q, k, v = (t.reshape(b, s, GDN_HEADS, GDN_DH) for t in jnp.split(qkv, 3, axis=-1))
    q = l2norm(q) * (GDN_DH ** -0.5)
    k = l2norm(k)
    beta = jax.nn.sigmoid(beta_raw.reshape(b, s, 2, GDN_HEADS))
    g = -jnp.exp(A_log) * jax.nn.softplus(a_raw.reshape(b, s, 2, GDN_HEADS) + dt_bias)
    o = gdn_dir(q, k, v, beta[:, :, 0], g[:, :, 0]) + flip(
        gdn_dir(flip(q), flip(k), flip(v), flip(beta[:, :, 1]), flip(g[:, :, 1])))
    o = rmsnorm(o, norm_w) * jax.nn.silu(gate.reshape(b, s, GDN_HEADS, GDN_DH))
    return o.reshape(b, s, GDN_W)


def ssd_dir(x, bm, cm, dt, la):
    b, s = x.shape[:2]
    x = _chunk(x * dt.reshape(b, s, SSD_GROUPS, SSD_HPG)[..., None])
    bm, cm = _chunk(bm), _chunk(cm)
    a_cum = jnp.cumsum(_chunk(la.reshape(b, s, SSD_GROUPS, SSD_HPG)), axis=2)
    ah = jnp.moveaxis(a_cum, 2, -1)
    L = _decay_matrix(ah)
    cb = jnp.einsum('bnigs,bnjgs->bngij', cm, bm)
    y = jnp.einsum('bngij,bnghij,bnjghp->bnighp', cb, L, x)
    a_tot = ah[..., -1]
    d_state = jnp.einsum('bnjgs,bnghj,bnjghp->bnghsp', bm, jnp.exp(a_tot[..., None] - ah), x)
    states = scan_states(jnp.exp(a_tot)[..., None, None], d_state)
    y = y + jnp.einsum('bnigs,bnghi,bnghsp->bnighp', cm, jnp.exp(ah), states)
    return _unchunk(y)


def ssd_mixer(z, xbc, dt_raw, conv_w, conv_b, A_log, dt_bias, D, norm_w):
    b, s = z.shape[:2]
    xbc = jax.nn.silu(dwconv(xbc, conv_w) + conv_b)
    xs, bm, cm = jnp.split(xbc, [SSD_DINNER, SSD_DINNER + SSD_GROUPS * SSD_DSTATE], axis=-1)
    xs = xs.reshape(b, s, SSD_GROUPS, SSD_HPG, SSD_HEADDIM)
    bm = bm.reshape(b, s, SSD_GROUPS, SSD_DSTATE)
    cm = cm.reshape(b, s, SSD_GROUPS, SSD_DSTATE)
    dt = jax.nn.softplus(dt_raw.reshape(b, s, 2, SSD_HEADS) + dt_bias)
    la = dt * (-jnp.exp(A_log))
    y = ssd_dir(xs, bm, cm, dt[:, :, 0], la[:, :, 0]) + flip(
        ssd_dir(flip(xs), flip(bm), flip(cm), flip(dt[:, :, 1]), flip(la[:, :, 1])))
    y = y + D.reshape(SSD_GROUPS, SSD_HPG)[..., None] * xs
    yz = (y.reshape(b, s, SSD_DINNER) * jax.nn.silu(z)).reshape(b, s, SSD_GROUPS, SSD_DINNER // SSD_GROUPS)
    y = rmsnorm(yz, norm_w.reshape(SSD_GROUPS, SSD_DINNER // SSD_GROUPS))
    return y.reshape(b, s, SSD_DINNER)


def encoder_layer(x, norm_mix_w, w_in, gla_gk_up, gla_gk_bias, gla_norm_w,
                  gdn_conv_w, gdn_A_log, gdn_dt_bias, gdn_norm_w,
                  ssd_conv_w, ssd_conv_b, ssd_A_log, ssd_dt_bias, ssd_D, ssd_norm_w,
                  w_out, norm_mlp_w, w_up, w_down):
    f32 = lambda t: t.astype(jnp.float32)
    h = rmsnorm(x, norm_mix_w)
    proj = f32(h @ w_in)
    split_points = [int(i) for i in np.cumsum(IN_SPLITS)[:-1]]
    (gla_q, gla_k, gla_v, gla_g, gla_lr, gdn_qkv, gdn_gate, gdn_beta, gdn_a,
     ssd_z, ssd_xbc, ssd_dt) = jnp.split(proj, split_points, axis=-1)
    o_gla = gla_mixer(gla_q, gla_k, gla_v, gla_g, gla_lr, f32(gla_gk_up), f32(gla_gk_bias), gla_norm_w)
    o_gdn = gdn_mixer(gdn_qkv, gdn_gate, gdn_beta, gdn_a, f32(gdn_conv_w), f32(gdn_A_log),
                      f32(gdn_dt_bias), gdn_norm_w)
    o_ssd = ssd_mixer(ssd_z, ssd_xbc, ssd_dt, f32(ssd_conv_w), f32(ssd_conv_b), f32(ssd_A_log),
                      f32(ssd_dt_bias), f32(ssd_D), ssd_norm_w)
    mix = jnp.concatenate([o_gla, o_gdn, o_ssd], axis=-1).astype(x.dtype)
    x = x + mix @ w_out
    h = rmsnorm(x, norm_mlp_w)
    x = x + jnp.square(jax.nn.relu(h @ w_up)) @ w_down
    return x


def _dt_bias_init(k, shape):
    dt = jnp.exp(jax.random.uniform(k, shape, minval=math.log(1e-3), maxval=math.log(1e-1)))
    return dt + jnp.log(-jnp.expm1(-dt))


def setup_inputs(seed: int = 0) -> dict:
    key = jax.random.key(seed)
    ks = jax.random.split(key, 22)
    nrm = jax.random.normal
    L = DEPTH
    gain = lambda k, shape: 1.0 + 0.02 * nrm(k, shape, jnp.float32)
    return {
        "x_prompt": nrm(ks[0], (BATCH, SEQ, D_MODEL), jnp.float32),
        "x_sample": nrm(ks[1], (DEC_BATCH, DEC_SEQ, D_MODEL), jnp.float32),
        "norm_mix_w": gain(ks[2], (L, D_MODEL)),
        "w_in": nrm(ks[3], (L, D_MODEL, D_IN_PROJ), jnp.float32) * D_MODEL ** -0.5,
        "gla_gk_up": nrm(ks[4], (L, 2, GLA_RANK, GLA_QK), jnp.float32) * GLA_RANK ** -0.5,
        "gla_gk_bias": 0.1 * nrm(ks[5], (L, 2, GLA_QK), jnp.float32),
        "gla_norm_w": gain(ks[6], (L, GLA_DV)),
        "gdn_conv_w": nrm(ks[7], (L, CONV_K, 3 * GDN_W), jnp.float32) * CONV_K ** -0.5,
        "gdn_A_log": jnp.log(jax.random.uniform(ks[8], (L, 2, GDN_HEADS), minval=1.0, maxval=16.0)),
        "gdn_dt_bias": _dt_bias_init(ks[9], (L, 2, GDN_HEADS)),
        "gdn_norm_w": gain(ks[10], (L, GDN_DH)),
        "ssd_conv_w": nrm(ks[11], (L, CONV_K, SSD_CONV_CH), jnp.float32) * CONV_K ** -0.5,
        "ssd_conv_b": 0.02 * nrm(ks[12], (L, SSD_CONV_CH), jnp.float32),
        "ssd_A_log": jnp.log(jax.random.uniform(ks[13], (L, 2, SSD_HEADS), minval=1.0, maxval=16.0)),
        "ssd_dt_bias": _dt_bias_init(ks[14], (L, 2, SSD_HEADS)),
        "ssd_D": gain(ks[15], (L, SSD_HEADS)),
        "ssd_norm_w": gain(ks[16], (L, SSD_DINNER)),
        "w_out": nrm(ks[17], (L, D_MIX, D_MODEL), jnp.float32) * D_MIX ** -0.5,
        "norm_mlp_w": gain(ks[18], (L, D_MODEL)),
        "w_up": nrm(ks[19], (L, D_MODEL, D_FF), jnp.float32) * D_MODEL ** -0.5,
        "w_down": nrm(ks[20], (L, D_FF, D_MODEL), jnp.float32) * D_FF ** -0.5,
        "norm_f_w": gain(ks[21], (D_MODEL,)),
    }


def reference(x_prompt, x_sample, norm_mix_w, w_in, gla_gk_up, gla_gk_bias, gla_norm_w,
              gdn_conv_w, gdn_A_log, gdn_dt_bias, gdn_norm_w,
              ssd_conv_w, ssd_conv_b, ssd_A_log, ssd_dt_bias, ssd_D, ssd_norm_w,
              w_out, norm_mlp_w, w_up, w_down, norm_f_w):
    layer_params = (norm_mix_w, w_in, gla_gk_up, gla_gk_bias, gla_norm_w,
                    gdn_conv_w, gdn_A_log, gdn_dt_bias, gdn_norm_w,
                    ssd_conv_w, ssd_conv_b, ssd_A_log, ssd_dt_bias, ssd_D, ssd_norm_w,
                    w_out, norm_mlp_w, w_up, w_down)

    def trunk(x):
        for l in range(DEPTH):
            x = encoder_layer(x, *[p[l] for p in layer_params])
        return rmsnorm(x, norm_f_w)

    y_prompt = trunk(x_prompt)
    y_sample = trunk(x_sample)
    return (y_prompt, y_sample)
```

```python
import functools

import numpy as np
import jax
import jax.numpy as jnp
from jax import lax
from jax.experimental import pallas as pl
from jax.experimental.pallas import tpu as pltpu

f32 = jnp.float32
bf16 = jnp.bfloat16

D_MODEL = 1024
EPS = 1e-6
CHUNK = 64
CONV_K = 5
D_FF = 4 * D_MODEL
GLA_HEADS, GLA_DK, GLA_DV, GLA_RANK, GLA_NORMALIZER = 4, 64, 128, 16, 16.0
GDN_HEADS, GDN_DH = 4, 128
SSD_DINNER, SSD_HEADDIM, SSD_HEADS, SSD_GROUPS, SSD_DSTATE = 1024, 64, 16, 2, 64
SSD_HPG = SSD_HEADS // SSD_GROUPS
GLA_QK = GLA_HEADS * GLA_DK
GLA_V = GLA_HEADS * GLA_DV
GDN_W = GDN_HEADS * GDN_DH
SSD_BC = SSD_GROUPS * SSD_DSTATE
SSD_CONV_CH = SSD_DINNER + 2 * SSD_BC
D_MIX = GLA_V + GDN_W + SSD_DINNER
IN_SPLITS = (GLA_QK, GLA_QK, GLA_V, GLA_V, 2 * GLA_RANK,
             3 * GDN_W, GDN_W, 2 * GDN_HEADS, 2 * GDN_HEADS,
             SSD_DINNER, SSD_CONV_CH, 2 * SSD_HEADS)

SMALL_W = 128
LR_OFF, BETA_OFF, A_OFF, DT_OFF = 0, 32, 40, 48
GLA_W = 2 * GLA_QK + 2 * GLA_V
GDN_IN_W = 4 * GDN_W
SSD_IN_W = SSD_DINNER + SSD_CONV_CH
PROJ_W = GLA_W + GDN_IN_W + SSD_IN_W + SMALL_W

V7X_SUBLANES = 8
HALO = V7X_SUBLANES
VMEM_LIMIT = 56 * 1024 * 1024
MIX_TB = 256
PROJ_TM = 512


def _dot(a, b):
    return jnp.dot(a, b, preferred_element_type=f32)


def _dot_t(a, b):
    return lax.dot_general(a, b, (((1,), (1,)), ((), ())), preferred_element_type=f32)


def _tdot(a, b):
    return lax.dot_general(a, b, (((0,), (0,)), ((), ())), preferred_element_type=f32)


def _split(x, n):
    parts, r = [], x
    for i in range(n):
        p = r.astype(bf16)
        parts.append(p)
        if i + 1 < n:
            r = r - p.astype(f32)
    return parts


def _dot_exact_left(m_bf, x, n=3):
    acc = None
    for p in _split(x, n):
        t = _dot(m_bf, p)
        acc = t if acc is None else acc + t
    return acc


def _dot_exact_right(x, m_bf, n=3):
    acc = None
    for p in _split(x, n):
        t = _dot(p, m_bf)
        acc = t if acc is None else acc + t
    return acc


def _silu(x):
    return x * jax.nn.sigmoid(x)


def _iota(shape, dim):
    return lax.broadcasted_iota(jnp.int32, shape, dim)


def _chunk_masks(width, rev):
    i = _iota((CHUNK, width), 0)
    j = _iota((CHUNK, width), 1) & (CHUNK - 1)
    if rev:
        return j >= i, j > i, j == i
    return j <= i, j < i, j == i


def _stack_heads(x, nheads, hw):
    lane = _iota(x.shape, 1)
    zero = jnp.zeros_like(x)
    return jnp.concatenate(
        [jnp.where((lane >= h * hw) & (lane < (h + 1) * hw), x, zero) for h in range(nheads)], axis=0)


def _chunk_order(nchunks, rev):
    return range(nchunks - 1, -1, -1) if rev else range(nchunks)


def _conv_silu(prev, cur, nxt, w_ref, bias, buf, first, last, tb):
    buf[0:HALO, :] = jnp.where(first, 0.0, prev)
    buf[HALO:HALO + tb, :] = cur
    buf[HALO + tb:2 * HALO + tb, :] = jnp.where(last, 0.0, nxt)
    acc = None
    for j in range(CONV_K):
        t = buf[pl.ds(HALO - CONV_K // 2 + j, tb), :] * w_ref[j:j + 1, :]
        acc = t if acc is None else acc + t
    if bias is not None:
        acc = acc + bias
    return _silu(acc)


def _inproj_body(x_ref, nw_ref, w_ref, gla_ref, gdn_ref, ssd_ref, small_ref):
    x = x_ref[...]
    h = (x * lax.rsqrt(jnp.mean(x * x, axis=-1, keepdims=True) + EPS) * nw_ref[...]).astype(bf16)
    off = 0
    for ref in (gla_ref, gdn_ref, ssd_ref, small_ref):
        w = ref.shape[-1]
        ref[...] = _dot(h, w_ref[:, off:off + w])
        off += w


def _inproj(x2d, norm_w, w_r):
    t = x2d.shape[0]
    tm = PROJ_TM
    widths = (GLA_W, GDN_IN_W, SSD_IN_W, SMALL_W)
    return pl.pallas_call(
        _inproj_body,
        grid=(t // tm,),
        in_specs=[pl.BlockSpec((tm, D_MODEL), lambda i: (i, 0)),
                  pl.BlockSpec((1, D_MODEL), lambda i: (0, 0)),
                  pl.BlockSpec((D_MODEL, PROJ_W), lambda i: (0, 0), pipeline_mode=pl.Buffered(1))],
        out_specs=[pl.BlockSpec((tm, w), lambda i: (i, 0)) for w in widths],
        out_shape=[jax.ShapeDtypeStruct((t, w), f32) for w in widths],
        compiler_params=pltpu.CompilerParams(dimension_semantics=("arbitrary",), vmem_limit_bytes=VMEM_LIMIT),
        name="inproj",
    )(x2d, norm_w.reshape(1, D_MODEL), w_r)


def _gla_body(rev, final, tb, *refs):
    if final:
        x_ref, small_ref, upw_ref, upb_ref, tri_ref, hsum_ref, obwd_ref, nw_ref, out_ref, st_ref, o_buf = refs
    else:
        x_ref, small_ref, upw_ref, upb_ref, tri_ref, out_ref, st_ref = refs
        o_buf = None

    @pl.when(pl.program_id(1) == 0)
    def _():
        st_ref[...] = jnp.zeros_like(st_ref)

    xin = x_ref[0]
    q = xin[:, 0:GLA_QK] * (GLA_DK ** -0.5)
    k = xin[:, GLA_QK:2 * GLA_QK]
    v_bf = xin[:, 2 * GLA_QK:2 * GLA_QK + GLA_V].astype(bf16)
    logits = _dot(small_ref[0].astype(bf16), upw_ref[...]) + upb_ref[...]
    gk = jax.nn.log_sigmoid(logits) / GLA_NORMALIZER
    g_cum = _dot_exact_left(tri_ref[...], gk)
    q_in = (q * jnp.exp(g_cum)).astype(bf16)
    k_in = (k * jnp.exp(-g_cum)).astype(bf16)

    incl, _, _ = _chunk_masks(CHUNK, rev)
    att_mask = jnp.concatenate([incl] * GLA_HEADS, axis=0)
    lane = _iota((GLA_DV, GLA_QK), 1)
    st = st_ref[...]
    for c in _chunk_order(tb // CHUNK, rev):
        r = slice(c * CHUNK, (c + 1) * CHUNK)
        last = c * CHUNK if rev else (c + 1) * CHUNK - 1
        g_tot = g_cum[last:last + 1, :]
        qm = _stack_heads(q_in[r], GLA_HEADS, GLA_DK)
        att = jnp.where(att_mask, _dot_t(qm, k_in[r]), 0.0).astype(bf16)
        st_bf = st.astype(bf16)
        for h in range(GLA_HEADS):
            hr = slice(h * CHUNK, (h + 1) * CHUNK)
            hv = slice(h * GLA_DV, (h + 1) * GLA_DV)
            o_h = _dot(att[hr], v_bf[r, hv]) + _dot_t(qm[hr], st_bf)
            if final:
                o_buf[r, hv] = o_h
            else:
                out_ref[0, r, hv] = o_h
        k_st = (k[r] * jnp.exp(g_tot - g_cum[r])).astype(bf16)
        upd = _tdot(v_bf[r], k_st)
        st = st * jnp.exp(g_tot)
        for h in range(GLA_HEADS):
            sel = (lane >= h * GLA_DK) & (lane < (h + 1) * GLA_DK)
            st = st + jnp.where(sel, upd[h * GLA_DV:(h + 1) * GLA_DV], 0.0)
    st_ref[...] = st

    if final:
        tot = o_buf[...] + obwd_ref[0]
        ms = _dot_exact_right(tot * tot, hsum_ref[...], 2) * (1.0 / GLA_DV)
        gate = xin[:, 2 * GLA_QK + GLA_V:]
        out_ref[0] = (tot * lax.rsqrt(ms + EPS) * nw_ref[...] * _silu(gate)).astype(out_ref.dtype)


def _mixer_index_maps(nb, rev, tb):
    hb = tb // HALO
    pos = (lambda i: nb - 1 - i) if rev else (lambda i: i)
    cur = lambda b, i: (b, pos(i), 0)
    prev = lambda b, i: (b, jnp.maximum(pos(i) * hb - 1, 0), 0)
    nxt = lambda b, i: (b, jnp.minimum((pos(i) + 1) * hb, nb * hb - 1), 0)
    const2 = lambda b, i: (0, 0)
    return pos, cur, prev, nxt, const2


def _gla_call(x3, small3, upw, upb, tri, hsum, obwd, nw, rev, final):
    bsz, s, _ = x3.shape
    tb = MIX_TB
    nb = s // tb
    _, cur, _, _, const2 = _mixer_index_maps(nb, rev, tb)
    in_specs = [pl.BlockSpec((1, tb, GLA_W), cur),
                pl.BlockSpec((1, tb, SMALL_W), cur),
                pl.BlockSpec((SMALL_W, GLA_QK), const2),
                pl.BlockSpec((1, GLA_QK), const2),
                pl.BlockSpec((tb, tb), const2)]
    args = [x3, small3, upw, upb, tri]
    scratch = [pltpu.VMEM((GLA_DV, GLA_QK), f32)]
    if final:
        in_specs += [pl.BlockSpec((GLA_V, GLA_V), const2),
                     pl.BlockSpec((1, tb, GLA_V), cur),
                     pl.BlockSpec((1, GLA_V), const2)]
        args += [hsum, obwd, nw]
        scratch += [pltpu.VMEM((tb, GLA_V), f32)]
    return pl.pallas_call(
        functools.partial(_gla_body, rev, final, tb),
        grid=(bsz, nb),
        in_specs=in_specs,
        out_specs=pl.BlockSpec((1, tb, GLA_V), cur),
        out_shape=jax.ShapeDtypeStruct((bsz, s, GLA_V), bf16 if final else f32),
        scratch_shapes=scratch,
        compiler_params=pltpu.CompilerParams(dimension_semantics=("arbitrary", "arbitrary"),
                                             vmem_limit_bytes=VMEM_LIMIT),
        name="gla_fwd" if final else "gla_bwd",
    )(*args)


def _gdn_body(rev, final, tb, *refs):
    (x_ref, prev_ref, next_ref, small_ref, convw_ref, alog_ref, dtb_ref,
     e64b_ref, e64g_ref, e128b_ref, e128g_ref, tri_ref, oblk_ref, hsum_ref) = refs[:14]
    if final:
        obwd_ref, nw_ref, out_ref, st_ref, cbuf, o_buf = refs[14:]
    else:
        out_ref, st_ref, cbuf = refs[14:]
        o_buf = None
    nb = pl.num_programs(1)
    i = pl.program_id(1)
    pos = nb - 1 - i if rev else i

    @pl.when(i == 0)
    def _():
        st_ref[...] = jnp.zeros_like(st_ref)

    xin = x_ref[0]
    w3 = 3 * GDN_W
    qkv = _conv_silu(prev_ref[0][:, 0:w3], xin[:, 0:w3], next_ref[0][:, 0:w3], convw_ref, None, cbuf,
                     pos == 0, pos == nb - 1, tb)
    q, k, v = qkv[:, 0:GDN_W], qkv[:, GDN_W:2 * GDN_W], qkv[:, 2 * GDN_W:]
    hsum = hsum_ref[...]
    q = q * lax.rsqrt(_dot_exact_right(q * q, hsum, 2) + EPS) * (GDN_DH ** -0.5)
    k = k * lax.rsqrt(_dot_exact_right(k * k, hsum, 2) + EPS)
    k_bf = k.astype(bf16)

    sm = small_ref[0]
    beta_all = jax.nn.sigmoid(sm)
    g_all = -jnp.exp(alog_ref[...]) * jax.nn.softplus(sm + dtb_ref[...])
    tri, oblk = tri_ref[...], oblk_ref[...]
    incl, strict, ident = _chunk_masks(GDN_HEADS * CHUNK, rev)
    ident_blk = jnp.concatenate([ident] * (tb // CHUNK), axis=0)
    gc64 = _dot_exact_left(tri, _dot_exact_right(g_all, e64g_ref[...]))
    gr64 = _dot_exact_left(oblk, jnp.where(ident_blk, gc64, 0.0))
    gc128 = _dot_exact_left(tri, _dot_exact_right(g_all, e128g_ref[...]))
    b64 = _dot_exact_right(beta_all, e64b_ref[...], 2)
    b128 = _dot_exact_right(beta_all, e128b_ref[...], 2)
    eg128 = jnp.exp(gc128)
    rhs_k = (k * b128 * eg128).astype(bf16)
    rhs_v = (v * b128).astype(bf16)
    q_bf = q.astype(bf16)
    q_dec = (q * eg128).astype(bf16)

    eye = jnp.where(ident, 1.0, 0.0)
    for c in _chunk_order(tb // CHUNK, rev):
        r = slice(c * CHUNK, (c + 1) * CHUNK)
        last = c * CHUNK if rev else (c + 1) * CHUNK - 1
        g_tot = gc128[last:last + 1, :]
        dec_u = jnp.exp(g_tot - gc128[r])
        e_tot = jnp.exp(g_tot)
        qk_l, kk_l = [], []
        for h in range(GDN_HEADS):
            hs = slice(h * GDN_DH, (h + 1) * GDN_DH)
            res = _dot_t(jnp.concatenate([q_bf[r, hs], k_bf[r, hs]], axis=0), k_bf[r, hs])
            qk_l.append(res[0:CHUNK])
            kk_l.append(res[CHUNK:])
        qk = jnp.concatenate(qk_l, axis=1)
        kk = jnp.concatenate(kk_l, axis=1)
        decay = jnp.exp(jnp.where(incl, gc64[r] - gr64[r], -jnp.inf))
        a_qk = qk * decay
        p = jnp.where(strict, -(b64[r] * kk * decay), 0.0)
        t_inv = eye + p
        p = _dot(p.astype(bf16), _stack_heads(p.astype(bf16), GDN_HEADS, CHUNK))
        n_doublings = 5
        for d in range(n_doublings):
            bd = _stack_heads(p.astype(bf16), GDN_HEADS, CHUNK)
            if d + 1 < n_doublings:
                res = _dot(jnp.concatenate([p, t_inv], axis=0).astype(bf16), bd)
                p = res[0:CHUNK]
                t_inv = t_inv + res[CHUNK:]
            else:
                t_inv = t_inv + _dot(t_inv.astype(bf16), bd)
        tm = _stack_heads(t_inv.astype(bf16), GDN_HEADS, CHUNK)
        r_all = jnp.concatenate(
            [jnp.concatenate([rhs_k[r, h * GDN_DH:(h + 1) * GDN_DH], rhs_v[r, h * GDN_DH:(h + 1) * GDN_DH]], axis=1)
             for h in range(GDN_HEADS)], axis=0)
        sol = _dot(tm, r_all)
        u_l, qs_l = [], []
        for h in range(GDN_HEADS):
            hs = slice(h * GDN_DH, (h + 1) * GDN_DH)
            hr = slice(h * CHUNK, (h + 1) * CHUNK)
            s_h = st_ref[h]
            wq = jnp.concatenate([sol[hr, 0:GDN_DH].astype(bf16), q_dec[r, hs]], axis=0)
            res = _dot(wq, s_h.astype(bf16))
            u = sol[hr, GDN_DH:] - res[0:CHUNK]
            qs_l.append(res[CHUNK:])
            u_l.append(u.astype(bf16))
            st_ref[h] = e_tot[:, hs] * s_h + _tdot(k_bf[r, hs], (u * dec_u[:, hs]).astype(bf16))
        o_intra = _dot(_stack_heads(a_qk.astype(bf16), GDN_HEADS, CHUNK), jnp.concatenate(u_l, axis=0))
        for h in range(GDN_HEADS):
            hs = slice(h * GDN_DH, (h + 1) * GDN_DH)
            o_h = o_intra[h * CHUNK:(h + 1) * CHUNK] + qs_l[h]
            if final:
                o_buf[r, hs] = o_h
            else:
                out_ref[0, r, hs] = o_h

    if final:
        tot = o_buf[...] + obwd_ref[0]
        ms = _dot_exact_right(tot * tot, hsum, 2) * (1.0 / GDN_DH)
        gate = xin[:, w3:]
        out_ref[0] = (tot * lax.rsqrt(ms + EPS) * nw_ref[...] * _silu(gate)).astype(out_ref.dtype)


def _gdn_call(x3, small3, convw, alog_row, dtb_row, consts, obwd, nw, rev, final):
    bsz, s, _ = x3.shape
    tb = MIX_TB
    nb = s // tb
    _, cur, prev, nxt, const2 = _mixer_index_maps(nb, rev, tb)
    e64b, e64g, e128b, e128g, tri, oblk, hsum = consts
    in_specs = [pl.BlockSpec((1, tb, GDN_IN_W), cur),
                pl.BlockSpec((1, HALO, GDN_IN_W), prev),
                pl.BlockSpec((1, HALO, GDN_IN_W), nxt),
                pl.BlockSpec((1, tb, SMALL_W), cur),
                pl.BlockSpec((CONV_K, 3 * GDN_W), const2),
                pl.BlockSpec((1, SMALL_W), const2),
                pl.BlockSpec((1, SMALL_W), const2),
                pl.BlockSpec(e64b.shape, const2),
                pl.BlockSpec(e64g.shape, const2),
                pl.BlockSpec(e128b.shape, const2),
                pl.BlockSpec(e128g.shape, const2),
                pl.BlockSpec((tb, tb), const2),
                pl.BlockSpec((tb, tb), const2),
                pl.BlockSpec(hsum.shape, const2)]
    args = [x3, x3, x3, small3, convw, alog_row, dtb_row, e64b, e64g, e128b, e128g, tri, oblk, hsum]
    scratch = [pltpu.VMEM((GDN_HEADS, GDN_DH, GDN_DH), f32),
               pltpu.VMEM((tb + 2 * HALO, 3 * GDN_W), f32)]
    if final:
        in_specs += [pl.BlockSpec((1, tb, GDN_W), cur), pl.BlockSpec((1, GDN_W), const2)]
        args += [obwd, nw]
        scratch += [pltpu.VMEM((tb, GDN_W), f32)]
    return pl.pallas_call(
        functools.partial(_gdn_body, rev, final, tb),
        grid=(bsz, nb),
        in_specs=in_specs,
        out_specs=pl.BlockSpec((1, tb, GDN_W), cur),
        out_shape=jax.ShapeDtypeStruct((bsz, s, GDN_W), bf16 if final else f32),
        scratch_shapes=scratch,
        compiler_params=pltpu.CompilerParams(dimension_semantics=("arbitrary", "arbitrary"),
                                             vmem_limit_bytes=VMEM_LIMIT),
        name="gdn_fwd" if final else "gdn_bwd",
    )(*args)


def _ssd_body(rev, final, tb, *refs):
    (x_ref, prev_ref, next_ref, small_ref, convw_ref, convb_ref, alog_ref, dtb_ref,
     e_ref, tri_ref, oblk_ref) = refs[:11]
    if final:
        ybwd_ref, d_ref, nw_ref, out_ref, st_ref, cbuf, y_buf = refs[11:]
    else:
        out_ref, st_ref, cbuf = refs[11:]
        y_buf = None
    nb = pl.num_programs(1)
    i = pl.program_id(1)
    pos = nb - 1 - i if rev else i

    @pl.when(i == 0)
    def _():
        st_ref[...] = jnp.zeros_like(st_ref)

    xin = x_ref[0]
    z0 = SSD_DINNER
    xbc = _conv_silu(prev_ref[0][:, z0:], xin[:, z0:], next_ref[0][:, z0:], convw_ref, convb_ref[...], cbuf,
                     pos == 0, pos == nb - 1, tb)
    xs = xbc[:, 0:SSD_DINNER]
    bm = xbc[:, SSD_DINNER:SSD_DINNER + SSD_BC].astype(bf16)
    cm = xbc[:, SSD_DINNER + SSD_BC:].astype(bf16)

    dt_all = jax.nn.softplus(small_ref[0] + dtb_ref[...])
    la_all = dt_all * (-jnp.exp(alog_ref[...]))
    e_mat, tri, oblk = e_ref[...], tri_ref[...], oblk_ref[...]
    x = xs * _dot_exact_right(dt_all, e_mat, 2)
    incl, _, ident = _chunk_masks(SSD_DINNER, rev)
    ident_blk = jnp.concatenate([ident] * (tb // CHUNK), axis=0)
    a_col = _dot_exact_left(tri, _dot_exact_right(la_all, e_mat))
    a_row = _dot_exact_left(oblk, jnp.where(ident_blk, a_col, 0.0))

    lane = _iota((CHUNK, 2 * SSD_HEADDIM), 1)
    half = SSD_DINNER // SSD_GROUPS
    for c in _chunk_order(tb // CHUNK, rev):
        r = slice(c * CHUNK, (c + 1) * CHUNK)
        last = c * CHUNK if rev else (c + 1) * CHUNK - 1
        a_c = a_col[r]
        a_tot = a_col[last:last + 1, :]
        decay = jnp.exp(jnp.where(incl, a_c - a_row[r], -jnp.inf))
        cb_l = []
        for g in range(SSD_GROUPS):
            gs = slice(g * SSD_DSTATE, (g + 1) * SSD_DSTATE)
            cb2 = _dot_t(cm[r, gs], jnp.concatenate([bm[r, gs], bm[r, gs]], axis=0))
            cb_l += [cb2] * (SSD_HPG // 2)
        w_att = (decay * jnp.concatenate(cb_l, axis=1)).astype(bf16)
        x_c = x[r]
        x_bf = x_c.astype(bf16)
        xw = (x_c * jnp.exp(a_tot - a_c)).astype(bf16)
        e_c = jnp.exp(a_c)
        e_tot = jnp.exp(a_tot)
        zero = jnp.zeros((CHUNK, 2 * SSD_HEADDIM), bf16)
        for pr in range(SSD_HEADS // 2):
            ps = slice(pr * 2 * SSD_HEADDIM, (pr + 1) * 2 * SSD_HEADDIM)
            xp = x_bf[:, ps]
            bd = jnp.concatenate([jnp.where(lane < SSD_HEADDIM, xp, zero),
                                  jnp.where(lane >= SSD_HEADDIM, xp, zero)], axis=0)
            g = pr // (SSD_HPG // 2)
            gs = slice(g * SSD_DSTATE, (g + 1) * SSD_DSTATE)
            gl = slice(g * half, (g + 1) * half)
            y_p = _dot(w_att[:, ps], bd)
            if pr % (SSD_HPG // 2) == 0:
                s_g = st_ref[g]
                y_state = _dot(cm[r, gs], s_g.astype(bf16)) * e_c[:, gl]
                st_ref[g] = e_tot[:, gl] * s_g + _tdot(bm[r, gs], xw[:, gl])
            lo = (pr % (SSD_HPG // 2)) * 2 * SSD_HEADDIM
            y_p = y_p + y_state[:, lo:lo + 2 * SSD_HEADDIM]
            if final:
                y_buf[r, ps] = y_p
            else:
                out_ref[0, r, ps] = y_p

    if final:
        y = y_buf[...] + ybwd_ref[0] + d_ref[...] * xs
        yz = y * _silu(xin[:, 0:SSD_DINNER])
        outs = []
        for g in range(SSD_GROUPS):
            t = yz[:, g * half:(g + 1) * half]
            outs.append(t * lax.rsqrt(jnp.mean(t * t, axis=-1, keepdims=True) + EPS))
        out_ref[0] = (jnp.concatenate(outs, axis=1) * nw_ref[...]).astype(out_ref.dtype)


def _ssd_call(x3, small3, convw, convb, alog_row, dtb_row, consts, ybwd, d_row, nw, rev, final):
    bsz, s, _ = x3.shape
    tb = MIX_TB
    nb = s // tb
    _, cur, prev, nxt, const2 = _mixer_index_maps(nb, rev, tb)
    e_mat, tri, oblk = consts
    in_specs = [pl.BlockSpec((1, tb, SSD_IN_W), cur),
                pl.BlockSpec((1, HALO, SSD_IN_W), prev),
                pl.BlockSpec((1, HALO, SSD_IN_W), nxt),
                pl.BlockSpec((1, tb, SMALL_W), cur),
                pl.BlockSpec((CONV_K, SSD_CONV_CH), const2),
                pl.BlockSpec((1, SSD_CONV_CH), const2),
                pl.BlockSpec((1, SMALL_W), const2),
                pl.BlockSpec((1, SMALL_W), const2),
                pl.BlockSpec(e_mat.shape, const2),
                pl.BlockSpec((tb, tb), const2),
                pl.BlockSpec((tb, tb), const2)]
    args = [x3, x3, x3, small3, convw, convb, alog_row, dtb_row, e_mat, tri, oblk]
    scratch = [pltpu.VMEM((SSD_GROUPS, SSD_DSTATE, SSD_HPG * SSD_HEADDIM), f32),
               pltpu.VMEM((tb + 2 * HALO, SSD_CONV_CH), f32)]
    if final:
        in_specs += [pl.BlockSpec((1, tb, SSD_DINNER), cur),
                     pl.BlockSpec((1, SSD_DINNER), const2),
                     pl.BlockSpec((1, SSD_DINNER), const2)]
        args += [ybwd, d_row, nw]
        scratch += [pltpu.VMEM((tb, SSD_DINNER), f32)]
    return pl.pallas_call(
        functools.partial(_ssd_body, rev, final, tb),
        grid=(bsz, nb),
        in_specs=in_specs,
        out_specs=pl.BlockSpec((1, tb, SSD_DINNER), cur),
        out_shape=jax.ShapeDtypeStruct((bsz, s, SSD_DINNER), bf16 if final else f32),
        scratch_shapes=scratch,
        compiler_params=pltpu.CompilerParams(dimension_semantics=("arbitrary", "arbitrary"),
                                             vmem_limit_bytes=VMEM_LIMIT),
        name="ssd_fwd" if final else "ssd_bwd",
    )(*args)


def _outmlp_body(last_layer, *refs):
    if last_layer:
        x_ref, og_ref, od_ref, os_ref, wo_ref, nw_ref, wu_ref, wd_ref, nf_ref, out_ref = refs
    else:
        x_ref, og_ref, od_ref, os_ref, wo_ref, nw_ref, wu_ref, wd_ref, out_ref = refs
    mix = (_dot(og_ref[...], wo_ref[0:GLA_V, :])
           + _dot(od_ref[...], wo_ref[GLA_V:GLA_V + GDN_W, :])
           + _dot(os_ref[...], wo_ref[GLA_V + GDN_W:, :]))
    x1 = x_ref[...] + mix
    h = (x1 * lax.rsqrt(jnp.mean(x1 * x1, axis=-1, keepdims=True) + EPS) * nw_ref[...]).astype(bf16)
    mlp = None
    ff_blk = D_FF // 4
    for j in range(4):
        up = jnp.maximum(_dot(h, wu_ref[:, j * ff_blk:(j + 1) * ff_blk]), 0.0)
        t = _dot((up * up).astype(bf16), wd_ref[j * ff_blk:(j + 1) * ff_blk, :])
        mlp = t if mlp is None else mlp + t
    acc = x1 + mlp
    if last_layer:
        acc = acc * lax.rsqrt(jnp.mean(acc * acc, axis=-1, keepdims=True) + EPS) * nf_ref[...]
    out_ref[...] = acc


def _outmlp(x2d, o_gla, o_gdn, o_ssd, w_out, norm_w, w_up, w_down, norm_f):
    t = x2d.shape[0]
    tm = PROJ_TM
    last_layer = norm_f is not None
    row = lambda i: (i, 0)
    const = lambda i: (0, 0)
    single = pl.Buffered(1)
    in_specs = [pl.BlockSpec((tm, D_MODEL), row),
                pl.BlockSpec((tm, GLA_V), row),
                pl.BlockSpec((tm, GDN_W), row),
                pl.BlockSpec((tm, SSD_DINNER), row),
                pl.BlockSpec((D_MIX, D_MODEL), const, pipeline_mode=single),
                pl.BlockSpec((1, D_MODEL), const),
                pl.BlockSpec((D_MODEL, D_FF), const, pipeline_mode=single),
                pl.BlockSpec((D_FF, D_MODEL), const, pipeline_mode=single)]
    args = [x2d, o_gla, o_gdn, o_ssd, w_out, norm_w.reshape(1, D_MODEL), w_up, w_down]
    if last_layer:
        in_specs.append(pl.BlockSpec((1, D_MODEL), const))
        args.append(norm_f.reshape(1, D_MODEL))
    return pl.pallas_call(
        functools.partial(_outmlp_body, last_layer),
        grid=(t // tm,),
        in_specs=in_specs,
        out_specs=pl.BlockSpec((tm, D_MODEL), row),
        out_shape=jax.ShapeDtypeStruct((t, D_MODEL), f32),
        compiler_params=pltpu.CompilerParams(dimension_semantics=("arbitrary",), vmem_limit_bytes=VMEM_LIMIT),
        name="outmlp_final" if last_layer else "outmlp",
    )(*args)


def _prep_w_in(w):
    s = np.concatenate([[0], np.cumsum(IN_SPLITS)])
    seg = lambda n: w[:, int(s[n]):int(s[n + 1])]
    pad = jnp.zeros((w.shape[0], SMALL_W - 2 * GLA_RANK - 4 * GDN_HEADS - 2 * SSD_HEADS), w.dtype)
    cols = [seg(0), seg(1), seg(2), seg(3),
            seg(5), seg(6),
            seg(9), seg(10),
            seg(4), seg(7), seg(8), seg(11), pad]
    return jnp.concatenate(cols, axis=1).astype(bf16)


def _place_row(vals, off):
    return jnp.zeros((1, SMALL_W), f32).at[0, off:off + vals.shape[0]].set(vals.astype(f32))


def _expand_matrix(off, nheads, width):
    m = np.zeros((SMALL_W, nheads * width), np.float32)
    for h in range(nheads):
        m[off + h, h * width:(h + 1) * width] = 1.0
    return jnp.asarray(m, bf16)


def _block_consts(tb, rev):
    r = np.arange(tb)
    same = (r[:, None] // CHUNK) == (r[None, :] // CHUNK)
    tri = same & ((r[None, :] >= r[:, None]) if rev else (r[None, :] <= r[:, None]))
    return jnp.asarray(tri, bf16), jnp.asarray(same, bf16)


def _head_sum_matrix(nheads, width):
    r = np.arange(nheads * width)
    return jnp.asarray((r[:, None] // width) == (r[None, :] // width), bf16)


def _layer(x3, p):
    bsz, s, _ = x3.shape
    assert s % MIX_TB == 0 and (bsz * s) % PROJ_TM == 0, (bsz, s)
    x2d = x3.reshape(bsz * s, D_MODEL)
    gla_in, gdn_in, ssd_in, small = _inproj(x2d, p["norm_mix_w"], p["w_in_r"])
    gla_in = gla_in.reshape(bsz, s, GLA_W)
    gdn_in = gdn_in.reshape(bsz, s, GDN_IN_W)
    ssd_in = ssd_in.reshape(bsz, s, SSD_IN_W)
    small = small.reshape(bsz, s, SMALL_W)

    outs = {}
    for name in ("gla", "gdn", "ssd"):
        prev = None
        for rev in (True, False):
            d = 1 if rev else 0
            final = not rev
            tri, oblk = _block_consts(MIX_TB, rev)
            if name == "gla":
                upw = jnp.zeros((SMALL_W, GLA_QK), f32).at[LR_OFF + d * GLA_RANK:LR_OFF + (d + 1) * GLA_RANK].set(
                    p["gla_gk_up"][d].astype(f32)).astype(bf16)
                upb = p["gla_gk_bias"][d].astype(f32).reshape(1, GLA_QK)
                prev = _gla_call(gla_in, small, upw, upb, tri, _head_sum_matrix(GLA_HEADS, GLA_DV), prev,
                                 jnp.tile(p["gla_norm_w"].astype(f32), GLA_HEADS).reshape(1, GLA_V), rev, final)
            elif name == "gdn":
                consts = (_expand_matrix(BETA_OFF + d * GDN_HEADS, GDN_HEADS, CHUNK),
                          _expand_matrix(A_OFF + d * GDN_HEADS, GDN_HEADS, CHUNK),
                          _expand_matrix(BETA_OFF + d * GDN_HEADS, GDN_HEADS, GDN_DH),
                          _expand_matrix(A_OFF + d * GDN_HEADS, GDN_HEADS, GDN_DH),
                          tri, oblk, _head_sum_matrix(GDN_HEADS, GDN_DH))
                prev = _gdn_call(gdn_in, small, p["gdn_conv_w"].astype(f32),
                                 _place_row(p["gdn_A_log"][d], A_OFF + d * GDN_HEADS),
                                 _place_row(p["gdn_dt_bias"][d], A_OFF + d * GDN_HEADS),
                                 consts, prev,
                                 jnp.tile(p["gdn_norm_w"].astype(f32), GDN_HEADS).reshape(1, GDN_W), rev, final)
            else:
                consts = (_expand_matrix(DT_OFF + d * SSD_HEADS, SSD_HEADS, SSD_HEADDIM), tri, oblk)
                prev = _ssd_call(ssd_in, small, p["ssd_conv_w"].astype(f32),
                                 p["ssd_conv_b"].astype(f32).reshape(1, SSD_CONV_CH),
                                 _place_row(p["ssd_A_log"][d], DT_OFF + d * SSD_HEADS),
                                 _place_row(p["ssd_dt_bias"][d], DT_OFF + d * SSD_HEADS),
                                 consts, prev,
                                 jnp.repeat(p["ssd_D"].astype(f32), SSD_HEADDIM).reshape(1, SSD_DINNER),
                                 p["ssd_norm_w"].astype(f32).reshape(1, SSD_DINNER), rev, final)
        outs[name] = prev.reshape(bsz * s, -1)

    y = _outmlp(x2d, outs["gla"], outs["gdn"], outs["ssd"], p["w_out"].astype(bf16), p["norm_mlp_w"],
                p["w_up"].astype(bf16), p["w_down"].astype(bf16), p.get("norm_f_w"))
    return y.reshape(bsz, s, D_MODEL)


def kernel(x_prompt, x_sample, norm_mix_w, w_in, gla_gk_up, gla_gk_bias, gla_norm_w, gdn_conv_w, gdn_A_log, gdn_dt_bias, gdn_norm_w, ssd_conv_w, ssd_conv_b, ssd_A_log, ssd_dt_bias, ssd_D, ssd_norm_w, w_out, norm_mlp_w, w_up, w_down, norm_f_w):
    stacked = dict(norm_mix_w=norm_mix_w, gla_gk_up=gla_gk_up, gla_gk_bias=gla_gk_bias, gla_norm_w=gla_norm_w,
                   gdn_conv_w=gdn_conv_w, gdn_A_log=gdn_A_log, gdn_dt_bias=gdn_dt_bias, gdn_norm_w=gdn_norm_w,
                   ssd_conv_w=ssd_conv_w, ssd_conv_b=ssd_conv_b, ssd_A_log=ssd_A_log, ssd_dt_bias=ssd_dt_bias,
                   ssd_D=ssd_D, ssd_norm_w=ssd_norm_w, w_out=w_out, norm_mlp_w=norm_mlp_w, w_up=w_up,
                   w_down=w_down)
    depth = w_in.shape[0]
    layers = []
    for l in range(depth):
        p = {k: v[l] for k, v in stacked.items()}
        p["w_in_r"] = _prep_w_in(w_in[l])
        if l == depth - 1:
            p["norm_f_w"] = norm_f_w
        layers.append(p)

    def trunk(x):
        for p in layers:
            x = _layer(x, p)
        return x

    return (trunk(x_prompt), trunk(x_sample))
```

```python
import functools

import numpy as np
import jax
import jax.numpy as jnp
from jax import lax
from jax.experimental import pallas as pl
from jax.experimental.pallas import tpu as pltpu

f32 = jnp.float32
bf16 = jnp.bfloat16

D_MODEL = 1024
EPS = 1e-6
CHUNK = 64
CONV_K = 5
D_FF = 4 * D_MODEL
GLA_HEADS, GLA_DK, GLA_DV, GLA_RANK, GLA_NORMALIZER = 4, 64, 128, 16, 16.0
GDN_HEADS, GDN_DH = 4, 128
SSD_DINNER, SSD_HEADDIM, SSD_HEADS, SSD_GROUPS, SSD_DSTATE = 1024, 64, 16, 2, 64
SSD_HPG = SSD_HEADS // SSD_GROUPS
GLA_QK = GLA_HEADS * GLA_DK
GLA_V = GLA_HEADS * GLA_DV
GDN_W = GDN_HEADS * GDN_DH
SSD_BC = SSD_GROUPS * SSD_DSTATE
SSD_CONV_CH = SSD_DINNER + 2 * SSD_BC
D_MIX = GLA_V + GDN_W + SSD_DINNER
IN_SPLITS = (GLA_QK, GLA_QK, GLA_V, GLA_V, 2 * GLA_RANK,
             3 * GDN_W, GDN_W, 2 * GDN_HEADS, 2 * GDN_HEADS,
             SSD_DINNER, SSD_CONV_CH, 2 * SSD_HEADS)

SMALL_W = 128
LR_OFF, BETA_OFF, A_OFF, DT_OFF = 0, 32, 40, 48
GLA_W = 2 * GLA_QK + 2 * GLA_V
GDN_IN_W = 4 * GDN_W
SSD_IN_W = SSD_DINNER + SSD_CONV_CH
PROJ_W = GLA_W + GDN_IN_W + SSD_IN_W + SMALL_W

V7X_SUBLANES = 8
HALO = V7X_SUBLANES
VMEM_LIMIT = 56 * 1024 * 1024
MIX_TB = 256
PROJ_TM = 512


def _dot(a, b):
    return jnp.dot(a, b, preferred_element_type=f32)


def _dot_t(a, b):
    return lax.dot_general(a, b, (((1,), (1,)), ((), ())), preferred_element_type=f32)


def _tdot(a, b):
    return lax.dot_general(a, b, (((0,), (0,)), ((), ())), preferred_element_type=f32)


def _split(x, n):
    parts, r = [], x
    for i in range(n):
        p = r.astype(bf16)
        parts.append(p)
        if i + 1 < n:
            r = r - p.astype(f32)
    return parts


def _dot_exact_left(m_bf, x, n=3):
    acc = None
    for p in _split(x, n):
        t = _dot(m_bf, p)
        acc = t if acc is None else acc + t
    return acc


def _expand(parts, e_bf):
    acc = None
    for p in parts:
        t = _dot(p, e_bf)
        acc = t if acc is None else acc + t
    return acc


def _silu(x):
    return x * jax.nn.sigmoid(x)


def _iota(shape, dim):
    return lax.broadcasted_iota(jnp.int32, shape, dim)


def _chunk_masks(width, rev):
    i = _iota((CHUNK, width), 0)
    j = _iota((CHUNK, width), 1) & (CHUNK - 1)
    if rev:
        return j >= i, j > i, j == i
    return j <= i, j < i, j == i


def _stack_heads(x, nheads, hw):
    lane = _iota(x.shape, 1)
    zero = jnp.zeros_like(x)
    return jnp.concatenate(
        [jnp.where((lane >= h * hw) & (lane < (h + 1) * hw), x, zero) for h in range(nheads)], axis=0)


def _chunk_order(nchunks, rev):
    return list(range(nchunks - 1, -1, -1)) if rev else list(range(nchunks))


def _scale_heads(x, nheads, hw, eps_scale_fn):
    outs = []
    for h in range(nheads):
        xh = x[:, h * hw:(h + 1) * hw]
        outs.append(xh * eps_scale_fn(jnp.sum(xh * xh, axis=-1, keepdims=True)))
    return jnp.concatenate(outs, axis=1)


def _l2norm_heads(x, nheads, hw, scale):
    return _scale_heads(x, nheads, hw, lambda ss: lax.rsqrt(ss + EPS) * scale)


def _rmsnorm_heads(x, nheads, hw):
    return _scale_heads(x, nheads, hw, lambda ss: lax.rsqrt(ss * (1.0 / hw) + EPS))


def _row_form(a_t, row0, nheads, cols):
    return jnp.concatenate([a_t[row0 + h:row0 + h + 1, cols] for h in range(nheads)], axis=1)


def _inproj_body(tiles_per_seq, x_ref, xp_ref, xn_ref, nw_ref, w_ref, gcw_ref, scw_ref, scb_ref,
                 gla_ref, gdn_ref, ssd_ref, small_ref, gbuf, sbuf):
    i = pl.program_id(0)
    tm = x_ref.shape[0]
    first = (i % tiles_per_seq) == 0
    last = (i % tiles_per_seq) == tiles_per_seq - 1
    rows = slice(HALO, HALO + tm)
    xe = jnp.concatenate([xp_ref[...], x_ref[...], xn_ref[...]], axis=0)
    h = (xe * lax.rsqrt(jnp.mean(xe * xe, axis=-1, keepdims=True) + EPS) * nw_ref[...]).astype(bf16)

    def conv_silu(buf, col0, width, cw_ref, bias):
        buf[...] = _dot(h, w_ref[:, col0:col0 + width])

        @pl.when(first)
        def _():
            buf[0:HALO, :] = jnp.zeros((HALO, width), f32)

        @pl.when(last)
        def _():
            buf[HALO + tm:, :] = jnp.zeros((HALO, width), f32)

        acc = None
        for j in range(CONV_K):
            t = buf[pl.ds(HALO - CONV_K // 2 + j, tm), :] * cw_ref[j:j + 1, :]
            acc = t if acc is None else acc + t
        if bias is not None:
            acc = acc + bias
        return _silu(acc)

    off = 0
    gla_ref[...] = _dot(h, w_ref[:, off:off + GLA_W])[rows]
    off += GLA_W
    gdn_ref[:, 0:3 * GDN_W] = conv_silu(gbuf, off, 3 * GDN_W, gcw_ref, None)
    off += 3 * GDN_W
    gdn_ref[:, 3 * GDN_W:] = _dot(h, w_ref[:, off:off + GDN_W])[rows]
    off += GDN_W
    ssd_ref[:, 0:SSD_DINNER] = _dot(h, w_ref[:, off:off + SSD_DINNER])[rows]
    off += SSD_DINNER
    ssd_ref[:, SSD_DINNER:] = conv_silu(sbuf, off, SSD_CONV_CH, scw_ref, scb_ref[...])
    off += SSD_CONV_CH
    small_ref[...] = _dot(h, w_ref[:, off:off + SMALL_W])[rows]


def _inproj(x2d, seq_len, norm_w, w_r, gdn_conv_w, ssd_conv_w, ssd_conv_b):
    t = x2d.shape[0]
    tm = PROJ_TM
    nt = t // tm
    hb = tm // HALO
    widths = (GLA_W, GDN_IN_W, SSD_IN_W, SMALL_W)
    const = lambda i: (0, 0)
    return pl.pallas_call(
        functools.partial(_inproj_body, seq_len // tm),
        grid=(nt,),
        in_specs=[pl.BlockSpec((tm, D_MODEL), lambda i: (i, 0)),
                  pl.BlockSpec((HALO, D_MODEL), lambda i: (jnp.maximum(i * hb - 1, 0), 0)),
                  pl.BlockSpec((HALO, D_MODEL), lambda i: (jnp.minimum((i + 1) * hb, nt * hb - 1), 0)),
                  pl.BlockSpec((1, D_MODEL), const),
                  pl.BlockSpec((D_MODEL, PROJ_W), const, pipeline_mode=pl.Buffered(1)),
                  pl.BlockSpec((CONV_K, 3 * GDN_W), const),
                  pl.BlockSpec((CONV_K, SSD_CONV_CH), const),
                  pl.BlockSpec((1, SSD_CONV_CH), const)],
        out_specs=[pl.BlockSpec((tm, w), lambda i: (i, 0)) for w in widths],
        out_shape=[jax.ShapeDtypeStruct((t, w), f32) for w in widths],
        scratch_shapes=[pltpu.VMEM((tm + 2 * HALO, 3 * GDN_W), f32),
                        pltpu.VMEM((tm + 2 * HALO, SSD_CONV_CH), f32)],
        compiler_params=pltpu.CompilerParams(dimension_semantics=("arbitrary",), vmem_limit_bytes=VMEM_LIMIT),
        name="inproj",
    )(x2d, x2d, x2d, norm_w.reshape(1, D_MODEL), w_r, gdn_conv_w, ssd_conv_w, ssd_conv_b)


def _gla_body(rev, final, tb, *refs):
    if final:
        x_ref, small_ref, upw_ref, upb_ref, tri_ref, obwd_ref, nw_ref, out_ref, st_ref, o_buf = refs
    else:
        x_ref, small_ref, upw_ref, upb_ref, tri_ref, out_ref, st_ref = refs
        o_buf = None
    chunks = _chunk_order(tb // CHUNK, rev)
    rows = {c: slice(c * CHUNK, (c + 1) * CHUNK) for c in chunks}

    @pl.when(pl.program_id(1) == 0)
    def _():
        st_ref[...] = jnp.zeros_like(st_ref)

    xin = x_ref[0]
    q = xin[:, 0:GLA_QK] * (GLA_DK ** -0.5)
    k = xin[:, GLA_QK:2 * GLA_QK]
    v_bf = xin[:, 2 * GLA_QK:2 * GLA_QK + GLA_V].astype(bf16)
    logits = _dot(small_ref[0].astype(bf16), upw_ref[...]) + upb_ref[...]
    gk = jax.nn.log_sigmoid(logits) / GLA_NORMALIZER
    g_cum = _dot_exact_left(tri_ref[...], gk)
    q_in = (q * jnp.exp(g_cum)).astype(bf16)
    k_in = (k * jnp.exp(-g_cum)).astype(bf16)

    incl, _, _ = _chunk_masks(CHUNK, rev)
    att_mask = jnp.concatenate([incl] * GLA_HEADS, axis=0)
    lane = _iota((GLA_DV, GLA_QK), 1)
    qm, att, upd, g_tot = {}, {}, {}, {}
    for c in chunks:
        r = rows[c]
        last = c * CHUNK if rev else (c + 1) * CHUNK - 1
        g_tot[c] = g_cum[last:last + 1, :]
        qm[c] = _stack_heads(q_in[r], GLA_HEADS, GLA_DK)
        att[c] = jnp.where(att_mask, _dot_t(qm[c], k_in[r]), 0.0).astype(bf16)
        k_st = (k[r] * jnp.exp(g_tot[c] - g_cum[r])).astype(bf16)
        upd[c] = _tdot(v_bf[r], k_st)
    st = st_ref[...]
    for c in chunks:
        r = rows[c]
        st_bf = st.astype(bf16)
        for h in range(GLA_HEADS):
            hr = slice(h * CHUNK, (h + 1) * CHUNK)
            hv = slice(h * GLA_DV, (h + 1) * GLA_DV)
            o_h = _dot(att[c][hr], v_bf[r, hv]) + _dot_t(qm[c][hr], st_bf)
            if final:
                o_buf[r, hv] = o_h
            else:
                out_ref[0, r, hv] = o_h
        st = st * jnp.exp(g_tot[c])
        for h in range(GLA_HEADS):
            sel = (lane >= h * GLA_DK) & (lane < (h + 1) * GLA_DK)
            st = st + jnp.where(sel, upd[c][h * GLA_DV:(h + 1) * GLA_DV], 0.0)
    st_ref[...] = st

    if final:
        tot = o_buf[...] + obwd_ref[0]
        gate = xin[:, 2 * GLA_QK + GLA_V:]
        out_ref[0] = (_rmsnorm_heads(tot, GLA_HEADS, GLA_DV) * nw_ref[...] * _silu(gate)).astype(out_ref.dtype)


def _mixer_index_maps(nb, rev):
    pos = (lambda i: nb - 1 - i) if rev else (lambda i: i)
    cur = lambda b, i: (b, pos(i), 0)
    const2 = lambda b, i: (0, 0)
    return cur, const2


def _mixer_call(body, name, x3, small3, params, extra_final, out_w, scratch, rev, final):
    bsz, s, in_w = x3.shape
    tb = MIX_TB
    nb = s // tb
    cur, const2 = _mixer_index_maps(nb, rev)
    in_specs = [pl.BlockSpec((1, tb, in_w), cur), pl.BlockSpec((1, tb, SMALL_W), cur)]
    in_specs += [pl.BlockSpec(p.shape, const2) for p in params]
    args = [x3, small3, *params]
    scratch = list(scratch)
    if final:
        o_other, consts = extra_final
        in_specs += [pl.BlockSpec((1, tb, out_w), cur)] + [pl.BlockSpec(p.shape, const2) for p in consts]
        args += [o_other, *consts]
        scratch += [pltpu.VMEM((tb, out_w), f32)]
    return pl.pallas_call(
        functools.partial(body, rev, final, tb),
        grid=(bsz, nb),
        in_specs=in_specs,
        out_specs=pl.BlockSpec((1, tb, out_w), cur),
        out_shape=jax.ShapeDtypeStruct((bsz, s, out_w), bf16 if final else f32),
        scratch_shapes=scratch,
        compiler_params=pltpu.CompilerParams(dimension_semantics=("arbitrary", "arbitrary"),
                                             vmem_limit_bytes=VMEM_LIMIT),
        name=name + ("_fwd" if final else "_bwd"),
    )(*args)


def _gdn_body(rev, final, tb, *refs):
    (x_ref, small_ref, alog_ref, dtb_ref, e64b_ref, e64g_ref, e128b_ref, e128g_ref, tri_ref) = refs[:9]
    if final:
        obwd_ref, nw_ref, out_ref, st_ref, o_buf = refs[9:]
    else:
        out_ref, st_ref = refs[9:]
        o_buf = None
    nheads, dh = GDN_HEADS, GDN_DH
    chunks = _chunk_order(tb // CHUNK, rev)
    rows = {c: slice(c * CHUNK, (c + 1) * CHUNK) for c in chunks}
    heads = [slice(h * dh, (h + 1) * dh) for h in range(nheads)]
    hrows = [slice(h * CHUNK, (h + 1) * CHUNK) for h in range(nheads)]

    @pl.when(pl.program_id(1) == 0)
    def _():
        st_ref[...] = jnp.zeros_like(st_ref)

    xin = x_ref[0]
    q = _l2norm_heads(xin[:, 0:GDN_W], nheads, dh, dh ** -0.5)
    k = _l2norm_heads(xin[:, GDN_W:2 * GDN_W], nheads, dh, 1.0)
    v = xin[:, 2 * GDN_W:3 * GDN_W]
    k_bf = k.astype(bf16)
    q_bf = q.astype(bf16)

    sm = small_ref[0]
    beta_parts = _split(jax.nn.sigmoid(sm), 2)
    g_all = -jnp.exp(alog_ref[...]) * jax.nn.softplus(sm + dtb_ref[...])
    g_cum = _dot_exact_left(tri_ref[...], g_all)
    g_parts = _split(g_cum, 3)
    gc64 = _expand(g_parts, e64g_ref[...])
    gc128 = _expand(g_parts, e128g_ref[...])
    b64 = _expand(beta_parts, e64b_ref[...])
    b128 = _expand(beta_parts, e128b_ref[...])
    g_cum_t = g_cum.T
    g_row0 = A_OFF + (nheads if rev else 0)
    eg128 = jnp.exp(gc128)
    rhs_k = (k * b128 * eg128).astype(bf16)
    rhs_v = (v * b128).astype(bf16)
    q_dec = (q * eg128).astype(bf16)

    incl, strict, ident = _chunk_masks(nheads * CHUNK, rev)
    eye = jnp.where(ident, 1.0, 0.0)

    qk, kk = {}, {}
    for c in chunks:
        qk_l, kk_l = [], []
        for hs in heads:
            res = _dot_t(jnp.concatenate([q_bf[rows[c], hs], k_bf[rows[c], hs]], axis=0), k_bf[rows[c], hs])
            qk_l.append(res[0:CHUNK])
            kk_l.append(res[CHUNK:])
        qk[c] = jnp.concatenate(qk_l, axis=1)
        kk[c] = jnp.concatenate(kk_l, axis=1)
    a_qk, p, t_inv = {}, {}, {}
    for c in chunks:
        g_row = _row_form(g_cum_t, g_row0, nheads, rows[c])
        decay = jnp.exp(jnp.where(incl, gc64[rows[c]] - g_row, -jnp.inf))
        a_qk[c] = _stack_heads((qk[c] * decay).astype(bf16), nheads, CHUNK)
        p[c] = jnp.where(strict, -(b64[rows[c]] * kk[c] * decay), 0.0)
        t_inv[c] = eye + p[c]
    for c in chunks:
        pb = p[c].astype(bf16)
        p[c] = _dot(pb, _stack_heads(pb, nheads, CHUNK))
    n_doublings = 5
    for d in range(n_doublings):
        for c in chunks:
            bd = _stack_heads(p[c].astype(bf16), nheads, CHUNK)
            if d + 1 < n_doublings:
                res = _dot(jnp.concatenate([p[c], t_inv[c]], axis=0).astype(bf16), bd)
                p[c] = res[0:CHUNK]
                t_inv[c] = t_inv[c] + res[CHUNK:]
            else:
                t_inv[c] = t_inv[c] + _dot(t_inv[c].astype(bf16), bd)
    sol = {}
    for c in chunks:
        tm = _stack_heads(t_inv[c].astype(bf16), nheads, CHUNK)
        r_all = jnp.concatenate(
            [jnp.concatenate([rhs_k[rows[c], hs], rhs_v[rows[c], hs]], axis=1) for hs in heads], axis=0)
        sol[c] = _dot(tm, r_all)

    s = [st_ref[h] for h in range(nheads)]
    for c in chunks:
        r = rows[c]
        last = c * CHUNK if rev else (c + 1) * CHUNK - 1
        g_tot = gc128[last:last + 1, :]
        dec_u = jnp.exp(g_tot - gc128[r])
        e_tot = jnp.exp(g_tot)
        res = [_dot(jnp.concatenate([sol[c][hrows[h], 0:dh].astype(bf16), q_dec[r, heads[h]]], axis=0),
                    s[h].astype(bf16)) for h in range(nheads)]
        u = [sol[c][hrows[h], dh:] - res[h][0:CHUNK] for h in range(nheads)]
        upd = [_tdot(k_bf[r, heads[h]], (u[h] * dec_u[:, heads[h]]).astype(bf16)) for h in range(nheads)]
        s = [e_tot[:, heads[h]] * s[h] + upd[h] for h in range(nheads)]
        o_intra = _dot(a_qk[c], jnp.concatenate([x.astype(bf16) for x in u], axis=0))
        for h in range(nheads):
            o_h = o_intra[hrows[h]] + res[h][CHUNK:]
            if final:
                o_buf[r, heads[h]] = o_h
            else:
                out_ref[0, r, heads[h]] = o_h
    for h in range(nheads):
        st_ref[h] = s[h]

    if final:
        tot = o_buf[...] + obwd_ref[0]
        gate = xin[:, 3 * GDN_W:]
        out_ref[0] = (_rmsnorm_heads(tot, nheads, dh) * nw_ref[...] * _silu(gate)).astype(out_ref.dtype)


def _ssd_body(rev, final, tb, *refs):
    (x_ref, small_ref, alog_ref, dtb_ref, e_ref, tri_ref) = refs[:6]
    if final:
        ybwd_ref, d_ref, nw_ref, out_ref, st_ref, y_buf = refs[6:]
    else:
        out_ref, st_ref = refs[6:]
        y_buf = None
    chunks = _chunk_order(tb // CHUNK, rev)
    rows = {c: slice(c * CHUNK, (c + 1) * CHUNK) for c in chunks}
    half = SSD_DINNER // SSD_GROUPS
    pair_w = 2 * SSD_HEADDIM
    pairs_per_group = SSD_HPG // 2

    @pl.when(pl.program_id(1) == 0)
    def _():
        st_ref[...] = jnp.zeros_like(st_ref)

    xin = x_ref[0]
    xs = xin[:, SSD_DINNER:2 * SSD_DINNER]
    bm = xin[:, 2 * SSD_DINNER:2 * SSD_DINNER + SSD_BC].astype(bf16)
    cm = xin[:, 2 * SSD_DINNER + SSD_BC:].astype(bf16)

    dt_all = jax.nn.softplus(small_ref[0] + dtb_ref[...])
    la_all = dt_all * (-jnp.exp(alog_ref[...]))
    e_mat = e_ref[...]
    x = xs * _expand(_split(dt_all, 2), e_mat)
    a_cum = _dot_exact_left(tri_ref[...], la_all)
    a_col = _expand(_split(a_cum, 3), e_mat)
    a_cum_t = a_cum.T
    a_row0 = DT_OFF + (SSD_HEADS if rev else 0)
    incl, _, _ = _chunk_masks(SSD_DINNER, rev)
    lane = _iota((CHUNK, pair_w), 1)
    zero = jnp.zeros((CHUNK, pair_w), bf16)

    w_att, x_bf, d_state, e_c, e_tot = {}, {}, {}, {}, {}
    for c in chunks:
        r = rows[c]
        last = c * CHUNK if rev else (c + 1) * CHUNK - 1
        a_c = a_col[r]
        a_tot = a_col[last:last + 1, :]
        a_row = _row_form(a_cum_t, a_row0, SSD_HEADS, r)
        decay = jnp.exp(jnp.where(incl, a_c - a_row, -jnp.inf))
        cb_l = []
        for g in range(SSD_GROUPS):
            gs = slice(g * SSD_DSTATE, (g + 1) * SSD_DSTATE)
            cb2 = _dot_t(cm[r, gs], jnp.concatenate([bm[r, gs], bm[r, gs]], axis=0))
            cb_l += [cb2] * pairs_per_group
        w_att[c] = (decay * jnp.concatenate(cb_l, axis=1)).astype(bf16)
        x_bf[c] = x[r].astype(bf16)
        xw = (x[r] * jnp.exp(a_tot - a_c)).astype(bf16)
        e_c[c] = jnp.exp(a_c)
        e_tot[c] = jnp.exp(a_tot)
        d_state[c] = [_tdot(bm[r, g * SSD_DSTATE:(g + 1) * SSD_DSTATE], xw[:, g * half:(g + 1) * half])
                      for g in range(SSD_GROUPS)]
    y_intra = {}
    for c in chunks:
        for pr in range(SSD_HEADS // 2):
            ps = slice(pr * pair_w, (pr + 1) * pair_w)
            xp = x_bf[c][:, ps]
            bd = jnp.concatenate([jnp.where(lane < SSD_HEADDIM, xp, zero),
                                  jnp.where(lane >= SSD_HEADDIM, xp, zero)], axis=0)
            y_intra[c, pr] = _dot(w_att[c][:, ps], bd)

    s = [st_ref[g] for g in range(SSD_GROUPS)]
    for c in chunks:
        r = rows[c]
        for g in range(SSD_GROUPS):
            gl = slice(g * half, (g + 1) * half)
            y_state = _dot(cm[r, g * SSD_DSTATE:(g + 1) * SSD_DSTATE], s[g].astype(bf16)) * e_c[c][:, gl]
            s[g] = e_tot[c][:, gl] * s[g] + d_state[c][g]
            for pp in range(pairs_per_group):
                pr = g * pairs_per_group + pp
                ps = slice(pr * pair_w, (pr + 1) * pair_w)
                y_p = y_intra[c, pr] + y_state[:, pp * pair_w:(pp + 1) * pair_w]
                if final:
                    y_buf[r, ps] = y_p
                else:
                    out_ref[0, r, ps] = y_p
    for g in range(SSD_GROUPS):
        st_ref[g] = s[g]

    if final:
        y = y_buf[...] + ybwd_ref[0] + d_ref[...] * xs
        yz = y * _silu(xin[:, 0:SSD_DINNER])
        out_ref[0] = (_rmsnorm_heads(yz, SSD_GROUPS, half) * nw_ref[...]).astype(out_ref.dtype)


def _outmlp_body(last_layer, *refs):
    if last_layer:
        x_ref, og_ref, od_ref, os_ref, wo_ref, nw_ref, wu_ref, wd_ref, nf_ref, out_ref = refs
    else:
        x_ref, og_ref, od_ref, os_ref, wo_ref, nw_ref, wu_ref, wd_ref, out_ref = refs
    mix = (_dot(og_ref[...], wo_ref[0:GLA_V, :])
           + _dot(od_ref[...], wo_ref[GLA_V:GLA_V + GDN_W, :])
           + _dot(os_ref[...], wo_ref[GLA_V + GDN_W:, :]))
    x1 = x_ref[...] + mix
    h = (x1 * lax.rsqrt(jnp.mean(x1 * x1, axis=-1, keepdims=True) + EPS) * nw_ref[...]).astype(bf16)
    mlp = None
    ff_blk = D_FF // 4
    for j in range(4):
        up = jnp.maximum(_dot(h, wu_ref[:, j * ff_blk:(j + 1) * ff_blk]), 0.0)
        t = _dot((up * up).astype(bf16), wd_ref[j * ff_blk:(j + 1) * ff_blk, :])
        mlp = t if mlp is None else mlp + t
    acc = x1 + mlp
    if last_layer:
        acc = acc * lax.rsqrt(jnp.mean(acc * acc, axis=-1, keepdims=True) + EPS) * nf_ref[...]
    out_ref[...] = acc


def _outmlp(x2d, o_gla, o_gdn, o_ssd, w_out, norm_w, w_up, w_down, norm_f):
    t = x2d.shape[0]
    tm = PROJ_TM
    last_layer = norm_f is not None
    row = lambda i: (i, 0)
    const = lambda i: (0, 0)
    single = pl.Buffered(1)
    in_specs = [pl.BlockSpec((tm, D_MODEL), row),
                pl.BlockSpec((tm, GLA_V), row),
                pl.BlockSpec((tm, GDN_W), row),
                pl.BlockSpec((tm, SSD_DINNER), row),
                pl.BlockSpec((D_MIX, D_MODEL), const, pipeline_mode=single),
                pl.BlockSpec((1, D_MODEL), const),
                pl.BlockSpec((D_MODEL, D_FF), const, pipeline_mode=single),
                pl.BlockSpec((D_FF, D_MODEL), const, pipeline_mode=single)]
    args = [x2d, o_gla, o_gdn, o_ssd, w_out, norm_w.reshape(1, D_MODEL), w_up, w_down]
    if last_layer:
        in_specs.append(pl.BlockSpec((1, D_MODEL), const))
        args.append(norm_f.reshape(1, D_MODEL))
    return pl.pallas_call(
        functools.partial(_outmlp_body, last_layer),
        grid=(t // tm,),
        in_specs=in_specs,
        out_specs=pl.BlockSpec((tm, D_MODEL), row),
        out_shape=jax.ShapeDtypeStruct((t, D_MODEL), f32),
        compiler_params=pltpu.CompilerParams(dimension_semantics=("arbitrary",), vmem_limit_bytes=VMEM_LIMIT),
        name="outmlp_final" if last_layer else "outmlp",
    )(*args)


def _prep_w_in(w):
    s = np.concatenate([[0], np.cumsum(IN_SPLITS)])
    seg = lambda n: w[:, int(s[n]):int(s[n + 1])]
    pad = jnp.zeros((w.shape[0], SMALL_W - 2 * GLA_RANK - 4 * GDN_HEADS - 2 * SSD_HEADS), w.dtype)
    cols = [seg(0), seg(1), seg(2), seg(3),
            seg(5), seg(6),
            seg(9), seg(10),
            seg(4), seg(7), seg(8), seg(11), pad]
    return jnp.concatenate(cols, axis=1).astype(bf16)


def _place_row(vals, off):
    return jnp.zeros((1, SMALL_W), f32).at[0, off:off + vals.shape[0]].set(vals.astype(f32))


def _expand_matrix(off, nheads, width):
    m = np.zeros((SMALL_W, nheads * width), np.float32)
    for h in range(nheads):
        m[off + h, h * width:(h + 1) * width] = 1.0
    return jnp.asarray(m, bf16)


def _cumsum_matrix(tb, rev):
    r = np.arange(tb)
    same = (r[:, None] // CHUNK) == (r[None, :] // CHUNK)
    tri = same & ((r[None, :] >= r[:, None]) if rev else (r[None, :] <= r[:, None]))
    return jnp.asarray(tri, bf16)


def _layer(x3, p):
    bsz, s, _ = x3.shape
    assert s % MIX_TB == 0 and s % PROJ_TM == 0, (bsz, s)
    x2d = x3.reshape(bsz * s, D_MODEL)
    gla_in, gdn_in, ssd_in, small = _inproj(
        x2d, s, p["norm_mix_w"], p["w_in_r"], p["gdn_conv_w"].astype(f32), p["ssd_conv_w"].astype(f32),
        p["ssd_conv_b"].astype(f32).reshape(1, SSD_CONV_CH))
    gla_in = gla_in.reshape(bsz, s, GLA_W)
    gdn_in = gdn_in.reshape(bsz, s, GDN_IN_W)
    ssd_in = ssd_in.reshape(bsz, s, SSD_IN_W)
    small = small.reshape(bsz, s, SMALL_W)

    outs = {}
    for name in ("gla", "gdn", "ssd"):
        prev = None
        for rev in (True, False):
            d = 1 if rev else 0
            final = not rev
            tri = _cumsum_matrix(MIX_TB, rev)
            if name == "gla":
                upw = jnp.zeros((SMALL_W, GLA_QK), f32).at[LR_OFF + d * GLA_RANK:LR_OFF + (d + 1) * GLA_RANK].set(
                    p["gla_gk_up"][d].astype(f32)).astype(bf16)
                upb = p["gla_gk_bias"][d].astype(f32).reshape(1, GLA_QK)
                nw = jnp.tile(p["gla_norm_w"].astype(f32), GLA_HEADS).reshape(1, GLA_V)
                prev = _mixer_call(_gla_body, "gla", gla_in, small, (upw, upb, tri), (prev, (nw,)), GLA_V,
                                   [pltpu.VMEM((GLA_DV, GLA_QK), f32)], rev, final)
            elif name == "gdn":
                params = (_place_row(p["gdn_A_log"][d], A_OFF + d * GDN_HEADS),
                          _place_row(p["gdn_dt_bias"][d], A_OFF + d * GDN_HEADS),
                          _expand_matrix(BETA_OFF + d * GDN_HEADS, GDN_HEADS, CHUNK),
                          _expand_matrix(A_OFF + d * GDN_HEADS, GDN_HEADS, CHUNK),
                          _expand_matrix(BETA_OFF + d * GDN_HEADS, GDN_HEADS, GDN_DH),
                          _expand_matrix(A_OFF + d * GDN_HEADS, GDN_HEADS, GDN_DH),
                          tri)
                nw = jnp.tile(p["gdn_norm_w"].astype(f32), GDN_HEADS).reshape(1, GDN_W)
                prev = _mixer_call(_gdn_body, "gdn", gdn_in, small, params, (prev, (nw,)), GDN_W,
                                   [pltpu.VMEM((GDN_HEADS, GDN_DH, GDN_DH), f32)], rev, final)
            else:
                params = (_place_row(p["ssd_A_log"][d], DT_OFF + d * SSD_HEADS),
                          _place_row(p["ssd_dt_bias"][d], DT_OFF + d * SSD_HEADS),
                          _expand_matrix(DT_OFF + d * SSD_HEADS, SSD_HEADS, SSD_HEADDIM),
                          tri)
                d_row = jnp.repeat(p["ssd_D"].astype(f32), SSD_HEADDIM).reshape(1, SSD_DINNER)
                nw = p["ssd_norm_w"].astype(f32).reshape(1, SSD_DINNER)
                prev = _mixer_call(_ssd_body, "ssd", ssd_in, small, params, (prev, (d_row, nw)), SSD_DINNER,
                                   [pltpu.VMEM((SSD_GROUPS, SSD_DSTATE, SSD_HPG * SSD_HEADDIM), f32)], rev, final)
        outs[name] = prev.reshape(bsz * s, -1)

    y = _outmlp(x2d, outs["gla"], outs["gdn"], outs["ssd"], p["w_out"].astype(bf16), p["norm_mlp_w"],
                p["w_up"].astype(bf16), p["w_down"].astype(bf16), p.get("norm_f_w"))
    return y.reshape(bsz, s, D_MODEL)


def kernel(x_prompt, x_sample, norm_mix_w, w_in, gla_gk_up, gla_gk_bias, gla_norm_w, gdn_conv_w, gdn_A_log, gdn_dt_bias, gdn_norm_w, ssd_conv_w, ssd_conv_b, ssd_A_log, ssd_dt_bias, ssd_D, ssd_norm_w, w_out, norm_mlp_w, w_up, w_down, norm_f_w):
    stacked = dict(norm_mix_w=norm_mix_w, gla_gk_up=gla_gk_up, gla_gk_bias=gla_gk_bias, gla_norm_w=gla_norm_w,
                   gdn_conv_w=gdn_conv_w, gdn_A_log=gdn_A_log, gdn_dt_bias=gdn_dt_bias, gdn_norm_w=gdn_norm_w,
                   ssd_conv_w=ssd_conv_w, ssd_conv_b=ssd_conv_b, ssd_A_log=ssd_A_log, ssd_dt_bias=ssd_dt_bias,
                   ssd_D=ssd_D, ssd_norm_w=ssd_norm_w, w_out=w_out, norm_mlp_w=norm_mlp_w, w_up=w_up,
                   w_down=w_down)
    depth = w_in.shape[0]
    layers = []
    for l in range(depth):
        p = {k: v[l] for k, v in stacked.items()}
        p["w_in_r"] = _prep_w_in(w_in[l])
        if l == depth - 1:
            p["norm_f_w"] = norm_f_w
        layers.append(p)

    def trunk(x):
        for p in layers:
            x = _layer(x, p)
        return x

    return (trunk(x_prompt), trunk(x_sample))
```

```python
import functools

import numpy as np
import jax
import jax.numpy as jnp
from jax import lax
from jax.experimental import pallas as pl
from jax.experimental.pallas import tpu as pltpu

f32 = jnp.float32
bf16 = jnp.bfloat16

D_MODEL = 1024
EPS = 1e-6
CHUNK = 64
CONV_K = 5
D_FF = 4 * D_MODEL
GLA_HEADS, GLA_DK, GLA_DV, GLA_RANK, GLA_NORMALIZER = 4, 64, 128, 16, 16.0
GDN_HEADS, GDN_DH = 4, 128
SSD_DINNER, SSD_HEADDIM, SSD_HEADS, SSD_GROUPS, SSD_DSTATE = 1024, 64, 16, 2, 64
SSD_HPG = SSD_HEADS // SSD_GROUPS
GLA_QK = GLA_HEADS * GLA_DK
GLA_V = GLA_HEADS * GLA_DV
GDN_W = GDN_HEADS * GDN_DH
SSD_BC = SSD_GROUPS * SSD_DSTATE
SSD_CONV_CH = SSD_DINNER + 2 * SSD_BC
D_MIX = GLA_V + GDN_W + SSD_DINNER
IN_SPLITS = (GLA_QK, GLA_QK, GLA_V, GLA_V, 2 * GLA_RANK,
             3 * GDN_W, GDN_W, 2 * GDN_HEADS, 2 * GDN_HEADS,
             SSD_DINNER, SSD_CONV_CH, 2 * SSD_HEADS)

SMALL_W = 128
LR_OFF, BETA_OFF, A_OFF, DT_OFF = 0, 32, 40, 48
GLA_W = 2 * GLA_QK + 2 * GLA_V
GDN_IN_W = 4 * GDN_W
SSD_IN_W = SSD_DINNER + SSD_CONV_CH
PROJ_W = GLA_W + GDN_IN_W + SSD_IN_W + SMALL_W

V7X_SUBLANES = 8
HALO = V7X_SUBLANES
VMEM_LIMIT = 56 * 1024 * 1024
MIX_TB = {"gla": 512, "gdn": 512, "ssd": 256}
PROJ_TM = 512
CONV_ROWS, CONV_COLS = 64, 256
INV_BASE = 8
FIN_ROWS = 32


def _dot(a, b):
    return jnp.dot(a, b, preferred_element_type=f32)


def _dot_t(a, b):
    return lax.dot_general(a, b, (((1,), (1,)), ((), ())), preferred_element_type=f32)


def _tdot(a, b):
    return lax.dot_general(a, b, (((0,), (0,)), ((), ())), preferred_element_type=f32)


def _split(x, n):
    parts, r = [], x
    for i in range(n):
        p = r.astype(bf16)
        parts.append(p)
        if i + 1 < n:
            r = r - p.astype(f32)
    return parts


def _dot_exact_left(m_bf, x, n=3):
    acc = None
    for p in _split(x, n):
        t = _dot(m_bf, p)
        acc = t if acc is None else acc + t
    return acc


def _expand(parts, e_bf):
    acc = None
    for p in parts:
        t = _dot(p, e_bf)
        acc = t if acc is None else acc + t
    return acc


def _silu(x):
    return x * jax.nn.sigmoid(x)


def _iota(shape, dim):
    return lax.broadcasted_iota(jnp.int32, shape, dim)


def _chunk_masks(width, rev):
    i = _iota((CHUNK, width), 0)
    j = _iota((CHUNK, width), 1) & (CHUNK - 1)
    if rev:
        return j >= i, j > i, j == i
    return j <= i, j < i, j == i


def _stack_heads(x, nheads, hw):
    lane = _iota(x.shape, 1)
    zero = jnp.zeros_like(x)
    return jnp.concatenate(
        [jnp.where((lane >= h * hw) & (lane < (h + 1) * hw), x, zero) for h in range(nheads)], axis=0)


def _chunk_order(nchunks, rev):
    return list(range(nchunks - 1, -1, -1)) if rev else list(range(nchunks))


def _scale_heads(x, nheads, hw, eps_scale_fn):
    outs = []
    for h in range(nheads):
        xh = x[:, h * hw:(h + 1) * hw]
        outs.append(xh * eps_scale_fn(jnp.sum(xh * xh, axis=-1, keepdims=True)))
    return jnp.concatenate(outs, axis=1)


def _l2norm_heads(x, nheads, hw, scale):
    return _scale_heads(x, nheads, hw, lambda ss: lax.rsqrt(ss + EPS) * scale)


def _rmsnorm_heads(x, nheads, hw):
    return _scale_heads(x, nheads, hw, lambda ss: lax.rsqrt(ss * (1.0 / hw) + EPS))


def _head_matmul(l, r, nheads, passes):
    r_hi = r.astype(bf16)
    bd_hi = _stack_heads(r_hi, nheads, CHUNK)
    l_hi = l.astype(bf16)
    out = _dot(l_hi, bd_hi)
    if passes >= 2:
        out = out + _dot((l - l_hi.astype(f32)).astype(bf16), bd_hi)
    if passes >= 3:
        out = out + _dot(l_hi, _stack_heads((r - r_hi.astype(f32)).astype(bf16), nheads, CHUNK))
    return out


def _unit_triangular_inverse(a_neg, chunks, eye, nheads):
    i = _iota(eye.shape, 0)
    j = _iota(eye.shape, 1) & (CHUNK - 1)
    same_block = lambda b: (i >> (b.bit_length() - 1)) == (j >> (b.bit_length() - 1))
    base = same_block(INV_BASE)
    pw, t = {}, {}
    for c in chunks:
        pw[c] = jnp.where(base, a_neg[c], 0.0)
        t[c] = eye + pw[c]
    for c in chunks:
        pw[c] = _head_matmul(pw[c], pw[c], nheads, 3)
    for c in chunks:
        res = _head_matmul(jnp.concatenate([pw[c], t[c]], axis=0), pw[c], nheads, 3)
        pw[c] = res[0:CHUNK]
        t[c] = t[c] + res[CHUNK:]
    for c in chunks:
        t[c] = t[c] + _head_matmul(t[c], pw[c], nheads, 3)
    b = INV_BASE
    while b < CHUNK:
        off = same_block(2 * b) & jnp.logical_not(same_block(b))
        z = {c: _head_matmul(jnp.where(off, a_neg[c], 0.0), t[c], nheads, 1) for c in chunks}
        for c in chunks:
            t[c] = t[c] + _head_matmul(t[c], z[c], nheads, 1)
        b *= 2
    return t


def _row_form(a_t, row0, nheads, cols):
    return jnp.concatenate([a_t[row0 + h:row0 + h + 1, cols] for h in range(nheads)], axis=1)


def _inproj_body(tiles_per_seq, x_ref, xp_ref, xn_ref, nw_ref, w_ref, gcw_ref, scw_ref, scb_ref,
                 gla_ref, gdn_ref, ssd_ref, small_ref, gbuf, sbuf):
    i = pl.program_id(0)
    tm = x_ref.shape[0]
    first = (i % tiles_per_seq) == 0
    last = (i % tiles_per_seq) == tiles_per_seq - 1
    rows = slice(HALO, HALO + tm)
    xe = jnp.concatenate([xp_ref[...], x_ref[...], xn_ref[...]], axis=0)
    h = (xe * lax.rsqrt(jnp.mean(xe * xe, axis=-1, keepdims=True) + EPS) * nw_ref[...]).astype(bf16)

    c_gla, c_qkv = 0, GLA_W
    c_gate = c_qkv + 3 * GDN_W
    c_z = c_gate + GDN_W
    c_xbc = c_z + SSD_DINNER
    c_small = c_xbc + SSD_CONV_CH

    def plain(out_ref, out_col, w_col, width):
        out_ref[:, out_col:out_col + width] = _dot(h, w_ref[:, w_col:w_col + width])[rows].astype(out_ref.dtype)

    def conv_silu(buf, w_col, cw_ref, b_ref, out_ref, out_col):
        width = buf.shape[1]
        buf[...] = _dot(h, w_ref[:, w_col:w_col + width])

        @pl.when(first)
        def _():
            buf[0:HALO, :] = jnp.zeros((HALO, width), f32)

        @pl.when(last)
        def _():
            buf[HALO + tm:, :] = jnp.zeros((HALO, width), f32)

        for r0 in range(0, tm, CONV_ROWS):
            for c0 in range(0, width, CONV_COLS):
                cs = slice(c0, c0 + CONV_COLS)
                acc = None
                for j in range(CONV_K):
                    t = buf[pl.ds(HALO - CONV_K // 2 + j + r0, CONV_ROWS), cs] * cw_ref[j:j + 1, cs]
                    acc = t if acc is None else acc + t
                if b_ref is not None:
                    acc = acc + b_ref[:, cs]
                out_ref[r0:r0 + CONV_ROWS, out_col + c0:out_col + c0 + CONV_COLS] = _silu(acc).astype(out_ref.dtype)

    plain(gla_ref, 0, c_gla, GLA_W)
    conv_silu(gbuf, c_qkv, gcw_ref, None, gdn_ref, 0)
    plain(gdn_ref, 3 * GDN_W, c_gate, GDN_W)
    plain(ssd_ref, 0, c_z, SSD_DINNER)
    conv_silu(sbuf, c_xbc, scw_ref, scb_ref, ssd_ref, SSD_DINNER)
    plain(small_ref, 0, c_small, SMALL_W)


def _inproj(x2d, seq_len, norm_w, w_r, gdn_conv_w, ssd_conv_w, ssd_conv_b):
    t = x2d.shape[0]
    tm = PROJ_TM
    nt = t // tm
    hb = tm // HALO
    widths = (GLA_W, GDN_IN_W, SSD_IN_W, SMALL_W)
    const = lambda i: (0, 0)
    return pl.pallas_call(
        functools.partial(_inproj_body, seq_len // tm),
        grid=(nt,),
        in_specs=[pl.BlockSpec((tm, D_MODEL), lambda i: (i, 0)),
                  pl.BlockSpec((HALO, D_MODEL), lambda i: (jnp.maximum(i * hb - 1, 0), 0)),
                  pl.BlockSpec((HALO, D_MODEL), lambda i: (jnp.minimum((i + 1) * hb, nt * hb - 1), 0)),
                  pl.BlockSpec((1, D_MODEL), const),
                  pl.BlockSpec((D_MODEL, PROJ_W), const, pipeline_mode=pl.Buffered(1)),
                  pl.BlockSpec((CONV_K, 3 * GDN_W), const),
                  pl.BlockSpec((CONV_K, SSD_CONV_CH), const),
                  pl.BlockSpec((1, SSD_CONV_CH), const)],
        out_specs=[pl.BlockSpec((tm, w), lambda i: (i, 0)) for w in widths],
        out_shape=[jax.ShapeDtypeStruct((t, w), f32 if w == SMALL_W else bf16) for w in widths],
        scratch_shapes=[pltpu.VMEM((tm + 2 * HALO, 3 * GDN_W), f32),
                        pltpu.VMEM((tm + 2 * HALO, SSD_CONV_CH), f32)],
        compiler_params=pltpu.CompilerParams(dimension_semantics=("arbitrary",), vmem_limit_bytes=VMEM_LIMIT),
        name="inproj",
    )(x2d, x2d, x2d, norm_w.reshape(1, D_MODEL), w_r, gdn_conv_w, ssd_conv_w, ssd_conv_b)


def _gla_body(rev, final, tb, *refs):
    if final:
        x_ref, small_ref, upw_ref, upb_ref, tri_ref, obwd_ref, nw_ref, out_ref, st_ref, o_buf = refs
    else:
        x_ref, small_ref, upw_ref, upb_ref, tri_ref, out_ref, st_ref = refs
        o_buf = None
    chunks = _chunk_order(tb // CHUNK, rev)
    rows = {c: slice(c * CHUNK, (c + 1) * CHUNK) for c in chunks}

    @pl.when(pl.program_id(1) == 0)
    def _():
        st_ref[...] = jnp.zeros_like(st_ref)

    q = x_ref[0, :, 0:GLA_QK].astype(f32) * (GLA_DK ** -0.5)
    k = x_ref[0, :, GLA_QK:2 * GLA_QK].astype(f32)
    v_bf = x_ref[0, :, 2 * GLA_QK:2 * GLA_QK + GLA_V].astype(bf16)
    logits = _dot(small_ref[0].astype(bf16), upw_ref[...]) + upb_ref[...]
    gk = jax.nn.log_sigmoid(logits) / GLA_NORMALIZER
    g_cum = _dot_exact_left(tri_ref[...], gk)
    q_in = (q * jnp.exp(g_cum)).astype(bf16)
    k_in = (k * jnp.exp(-g_cum)).astype(bf16)

    incl, _, _ = _chunk_masks(CHUNK, rev)
    att_mask = jnp.concatenate([incl] * GLA_HEADS, axis=0)
    lane = _iota((GLA_DV, GLA_QK), 1)
    qm, att, upd, g_tot = {}, {}, {}, {}
    for c in chunks:
        r = rows[c]
        last = c * CHUNK if rev else (c + 1) * CHUNK - 1
        g_tot[c] = g_cum[last:last + 1, :]
        qm[c] = _stack_heads(q_in[r], GLA_HEADS, GLA_DK)
        att[c] = jnp.where(att_mask, _dot_t(qm[c], k_in[r]), 0.0).astype(bf16)
        k_st = (k[r] * jnp.exp(g_tot[c] - g_cum[r])).astype(bf16)
        upd[c] = _tdot(v_bf[r], k_st)
    st = st_ref[...]
    for c in chunks:
        r = rows[c]
        st_bf = st.astype(bf16)
        for h in range(GLA_HEADS):
            hr = slice(h * CHUNK, (h + 1) * CHUNK)
            hv = slice(h * GLA_DV, (h + 1) * GLA_DV)
            o_h = _dot(att[c][hr], v_bf[r, hv]) + _dot_t(qm[c][hr], st_bf)
            if final:
                o_buf[r, hv] = o_h
            else:
                out_ref[0, r, hv] = o_h
        st = st * jnp.exp(g_tot[c])
        for h in range(GLA_HEADS):
            sel = (lane >= h * GLA_DK) & (lane < (h + 1) * GLA_DK)
            st = st + jnp.where(sel, upd[c][h * GLA_DV:(h + 1) * GLA_DV], 0.0)
    st_ref[...] = st

    if final:
        for r0 in range(0, tb, FIN_ROWS):
            rr = slice(r0, r0 + FIN_ROWS)
            tot = o_buf[rr, :] + obwd_ref[0, rr, :]
            gate = x_ref[0, rr, 2 * GLA_QK + GLA_V:].astype(f32)
            out_ref[0, rr, :] = (_rmsnorm_heads(tot, GLA_HEADS, GLA_DV) * nw_ref[...]
                                 * _silu(gate)).astype(out_ref.dtype)


def _mixer_index_maps(nb, rev):
    pos = (lambda i: nb - 1 - i) if rev else (lambda i: i)
    cur = lambda b, i: (b, pos(i), 0)
    const2 = lambda b, i: (0, 0)
    return cur, const2


def _mixer_call(body, name, x3, small3, params, extra_final, out_w, scratch, rev, final):
    bsz, s, in_w = x3.shape
    tb = MIX_TB[name]
    nb = s // tb
    cur, const2 = _mixer_index_maps(nb, rev)
    in_specs = [pl.BlockSpec((1, tb, in_w), cur), pl.BlockSpec((1, tb, SMALL_W), cur)]
    in_specs += [pl.BlockSpec(p.shape, const2) for p in params]
    args = [x3, small3, *params]
    scratch = list(scratch)
    if final:
        o_other, consts = extra_final
        in_specs += [pl.BlockSpec((1, tb, out_w), cur)] + [pl.BlockSpec(p.shape, const2) for p in consts]
        args += [o_other, *consts]
        scratch += [pltpu.VMEM((tb, out_w), f32)]
    return pl.pallas_call(
        functools.partial(body, rev, final, tb),
        grid=(bsz, nb),
        in_specs=in_specs,
        out_specs=pl.BlockSpec((1, tb, out_w), cur),
        out_shape=jax.ShapeDtypeStruct((bsz, s, out_w), bf16 if final else f32),
        scratch_shapes=scratch,
        compiler_params=pltpu.CompilerParams(dimension_semantics=("arbitrary", "arbitrary"),
                                             vmem_limit_bytes=VMEM_LIMIT),
        name=name + ("_fwd" if final else "_bwd"),
    )(*args)


def _gdn_body(rev, final, tb, *refs):
    (x_ref, small_ref, alog_ref, dtb_ref, e64b_ref, e64g_ref, e128b_ref, e128g_ref, tri_ref) = refs[:9]
    if final:
        obwd_ref, nw_ref, out_ref, st_ref, o_buf = refs[9:]
    else:
        out_ref, st_ref = refs[9:]
        o_buf = None
    nheads, dh = GDN_HEADS, GDN_DH
    chunks = _chunk_order(tb // CHUNK, rev)
    rows = {c: slice(c * CHUNK, (c + 1) * CHUNK) for c in chunks}
    heads = [slice(h * dh, (h + 1) * dh) for h in range(nheads)]
    hrows = [slice(h * CHUNK, (h + 1) * CHUNK) for h in range(nheads)]

    @pl.when(pl.program_id(1) == 0)
    def _():
        st_ref[...] = jnp.zeros_like(st_ref)

    q = _l2norm_heads(x_ref[0, :, 0:GDN_W].astype(f32), nheads, dh, dh ** -0.5)
    k = _l2norm_heads(x_ref[0, :, GDN_W:2 * GDN_W].astype(f32), nheads, dh, 1.0)
    v = x_ref[0, :, 2 * GDN_W:3 * GDN_W].astype(f32)
    k_bf = k.astype(bf16)
    q_bf = q.astype(bf16)

    sm = small_ref[0]
    beta_parts = _split(jax.nn.sigmoid(sm), 2)
    g_all = -jnp.exp(alog_ref[...]) * jax.nn.softplus(sm + dtb_ref[...])
    g_cum = _dot_exact_left(tri_ref[...], g_all)
    g_parts = _split(g_cum, 3)
    gc64 = _expand(g_parts, e64g_ref[...])
    gc128 = _expand(g_parts, e128g_ref[...])
    b64 = _expand(beta_parts, e64b_ref[...])
    b128 = _expand(beta_parts, e128b_ref[...])
    g_cum_t = g_cum.T
    g_row0 = A_OFF + (nheads if rev else 0)
    eg128 = jnp.exp(gc128)
    rhs_k = (k * b128 * eg128).astype(bf16)
    rhs_v = (v * b128).astype(bf16)
    q_dec = (q * eg128).astype(bf16)

    incl, strict, ident = _chunk_masks(nheads * CHUNK, rev)
    eye = jnp.where(ident, 1.0, 0.0)

    qk, kk = {}, {}
    for c in chunks:
        qk_l, kk_l = [], []
        for hs in heads:
            res = _dot_t(jnp.concatenate([q_bf[rows[c], hs], k_bf[rows[c], hs]], axis=0), k_bf[rows[c], hs])
            qk_l.append(res[0:CHUNK])
            kk_l.append(res[CHUNK:])
        qk[c] = jnp.concatenate(qk_l, axis=1)
        kk[c] = jnp.concatenate(kk_l, axis=1)
    a_qk, a_neg = {}, {}
    for c in chunks:
        g_row = _row_form(g_cum_t, g_row0, nheads, rows[c])
        decay = jnp.exp(jnp.where(incl, gc64[rows[c]] - g_row, -jnp.inf))
        a_qk[c] = _stack_heads((qk[c] * decay).astype(bf16), nheads, CHUNK)
        a_neg[c] = jnp.where(strict, -(b64[rows[c]] * kk[c] * decay), 0.0)
    t_inv = _unit_triangular_inverse(a_neg, chunks, eye, nheads)
    sol = {}
    for c in chunks:
        tm = _stack_heads(t_inv[c].astype(bf16), nheads, CHUNK)
        r_all = jnp.concatenate(
            [jnp.concatenate([rhs_k[rows[c], hs], rhs_v[rows[c], hs]], axis=1) for hs in heads], axis=0)
        sol[c] = _dot(tm, r_all)

    s = [st_ref[h] for h in range(nheads)]
    for c in chunks:
        r = rows[c]
        last = c * CHUNK if rev else (c + 1) * CHUNK - 1
        g_tot = gc128[last:last + 1, :]
        dec_u = jnp.exp(g_tot - gc128[r])
        e_tot = jnp.exp(g_tot)
        res = [_dot(jnp.concatenate([sol[c][hrows[h], 0:dh].astype(bf16), q_dec[r, heads[h]]], axis=0),
                    s[h].astype(bf16)) for h in range(nheads)]
        u = [sol[c][hrows[h], dh:] - res[h][0:CHUNK] for h in range(nheads)]
        upd = [_tdot(k_bf[r, heads[h]], (u[h] * dec_u[:, heads[h]]).astype(bf16)) for h in range(nheads)]
        s = [e_tot[:, heads[h]] * s[h] + upd[h] for h in range(nheads)]
        o_intra = _dot(a_qk[c], jnp.concatenate([x.astype(bf16) for x in u], axis=0))
        for h in range(nheads):
            o_h = o_intra[hrows[h]] + res[h][CHUNK:]
            if final:
                o_buf[r, heads[h]] = o_h
            else:
                out_ref[0, r, heads[h]] = o_h
    for h in range(nheads):
        st_ref[h] = s[h]

    if final:
        for r0 in range(0, tb, FIN_ROWS):
            rr = slice(r0, r0 + FIN_ROWS)
            tot = o_buf[rr, :] + obwd_ref[0, rr, :]
            gate = x_ref[0, rr, 3 * GDN_W:].astype(f32)
            out_ref[0, rr, :] = (_rmsnorm_heads(tot, nheads, dh) * nw_ref[...]
                                 * _silu(gate)).astype(out_ref.dtype)


def _ssd_body(rev, final, tb, *refs):
    (x_ref, small_ref, alog_ref, dtb_ref, e_ref, tri_ref) = refs[:6]
    if final:
        ybwd_ref, d_ref, nw_ref, out_ref, st_ref, y_buf = refs[6:]
    else:
        out_ref, st_ref = refs[6:]
        y_buf = None
    chunks = _chunk_order(tb // CHUNK, rev)
    rows = {c: slice(c * CHUNK, (c + 1) * CHUNK) for c in chunks}
    half = SSD_DINNER // SSD_GROUPS
    pair_w = 2 * SSD_HEADDIM
    pairs_per_group = SSD_HPG // 2

    @pl.when(pl.program_id(1) == 0)
    def _():
        st_ref[...] = jnp.zeros_like(st_ref)

    xs = x_ref[0, :, SSD_DINNER:2 * SSD_DINNER].astype(f32)
    bm = x_ref[0, :, 2 * SSD_DINNER:2 * SSD_DINNER + SSD_BC].astype(bf16)
    cm = x_ref[0, :, 2 * SSD_DINNER + SSD_BC:].astype(bf16)

    dt_all = jax.nn.softplus(small_ref[0] + dtb_ref[...])
    la_all = dt_all * (-jnp.exp(alog_ref[...]))
    e_mat = e_ref[...]
    x = xs * _expand(_split(dt_all, 2), e_mat)
    a_cum = _dot_exact_left(tri_ref[...], la_all)
    a_col = _expand(_split(a_cum, 3), e_mat)
    a_cum_t = a_cum.T
    a_row0 = DT_OFF + (SSD_HEADS if rev else 0)
    incl, _, _ = _chunk_masks(SSD_DINNER, rev)
    lane = _iota((CHUNK, pair_w), 1)
    zero = jnp.zeros((CHUNK, pair_w), bf16)

    w_att, x_bf, d_state, e_c, e_tot = {}, {}, {}, {}, {}
    for c in chunks:
        r = rows[c]
        last = c * CHUNK if rev else (c + 1) * CHUNK - 1
        a_c = a_col[r]
        a_tot = a_col[last:last + 1, :]
        a_row = _row_form(a_cum_t, a_row0, SSD_HEADS, r)
        decay = jnp.exp(jnp.where(incl, a_c - a_row, -jnp.inf))
        cb_l = []
        for g in range(SSD_GROUPS):
            gs = slice(g * SSD_DSTATE, (g + 1) * SSD_DSTATE)
            cb2 = _dot_t(cm[r, gs], jnp.concatenate([bm[r, gs], bm[r, gs]], axis=0))
            cb_l += [cb2] * pairs_per_group
        w_att[c] = (decay * jnp.concatenate(cb_l, axis=1)).astype(bf16)
        x_bf[c] = x[r].astype(bf16)
        xw = (x[r] * jnp.exp(a_tot - a_c)).astype(bf16)
        e_c[c] = jnp.exp(a_c)
        e_tot[c] = jnp.exp(a_tot)
        d_state[c] = [_tdot(bm[r, g * SSD_DSTATE:(g + 1) * SSD_DSTATE], xw[:, g * half:(g + 1) * half])
                      for g in range(SSD_GROUPS)]
    y_intra = {}
    for c in chunks:
        for pr in range(SSD_HEADS // 2):
            ps = slice(pr * pair_w, (pr + 1) * pair_w)
            xp = x_bf[c][:, ps]
            bd = jnp.concatenate([jnp.where(lane < SSD_HEADDIM, xp, zero),
                                  jnp.where(lane >= SSD_HEADDIM, xp, zero)], axis=0)
            y_intra[c, pr] = _dot(w_att[c][:, ps], bd)

    s = [st_ref[g] for g in range(SSD_GROUPS)]
    for c in chunks:
        r = rows[c]
        for g in range(SSD_GROUPS):
            gl = slice(g * half, (g + 1) * half)
            y_state = _dot(cm[r, g * SSD_DSTATE:(g + 1) * SSD_DSTATE], s[g].astype(bf16)) * e_c[c][:, gl]
            s[g] = e_tot[c][:, gl] * s[g] + d_state[c][g]
            for pp in range(pairs_per_group):
                pr = g * pairs_per_group + pp
                ps = slice(pr * pair_w, (pr + 1) * pair_w)
                y_p = y_intra[c, pr] + y_state[:, pp * pair_w:(pp + 1) * pair_w]
                if final:
                    y_buf[r, ps] = y_p
                else:
                    out_ref[0, r, ps] = y_p
    for g in range(SSD_GROUPS):
        st_ref[g] = s[g]

    if final:
        for r0 in range(0, tb, FIN_ROWS // 2):
            rr = slice(r0, r0 + FIN_ROWS // 2)
            y = (y_buf[rr, :] + ybwd_ref[0, rr, :]
                 + d_ref[...] * x_ref[0, rr, SSD_DINNER:2 * SSD_DINNER].astype(f32))
            yz = y * _silu(x_ref[0, rr, 0:SSD_DINNER].astype(f32))
            out_ref[0, rr, :] = (_rmsnorm_heads(yz, SSD_GROUPS, half) * nw_ref[...]).astype(out_ref.dtype)


def _outmlp_body(last_layer, *refs):
    if last_layer:
        x_ref, og_ref, od_ref, os_ref, wo_ref, nw_ref, wu_ref, wd_ref, nf_ref, out_ref = refs
    else:
        x_ref, og_ref, od_ref, os_ref, wo_ref, nw_ref, wu_ref, wd_ref, out_ref = refs
    mix = (_dot(og_ref[...], wo_ref[0:GLA_V, :])
           + _dot(od_ref[...], wo_ref[GLA_V:GLA_V + GDN_W, :])
           + _dot(os_ref[...], wo_ref[GLA_V + GDN_W:, :]))
    x1 = x_ref[...] + mix
    h = (x1 * lax.rsqrt(jnp.mean(x1 * x1, axis=-1, keepdims=True) + EPS) * nw_ref[...]).astype(bf16)
    mlp = None
    ff_blk = D_FF // 4
    for j in range(4):
        up = jnp.maximum(_dot(h, wu_ref[:, j * ff_blk:(j + 1) * ff_blk]), 0.0)
        t = _dot((up * up).astype(bf16), wd_ref[j * ff_blk:(j + 1) * ff_blk, :])
        mlp = t if mlp is None else mlp + t
    acc = x1 + mlp
    if last_layer:
        acc = acc * lax.rsqrt(jnp.mean(acc * acc, axis=-1, keepdims=True) + EPS) * nf_ref[...]
    out_ref[...] = acc


def _outmlp(x2d, o_gla, o_gdn, o_ssd, w_out, norm_w, w_up, w_down, norm_f):
    t = x2d.shape[0]
    tm = PROJ_TM
    last_layer = norm_f is not None
    row = lambda i: (i, 0)
    const = lambda i: (0, 0)
    single = pl.Buffered(1)
    in_specs = [pl.BlockSpec((tm, D_MODEL), row),
                pl.BlockSpec((tm, GLA_V), row),
                pl.BlockSpec((tm, GDN_W), row),
                pl.BlockSpec((tm, SSD_DINNER), row),
                pl.BlockSpec((D_MIX, D_MODEL), const, pipeline_mode=single),
                pl.BlockSpec((1, D_MODEL), const),
                pl.BlockSpec((D_MODEL, D_FF), const, pipeline_mode=single),
                pl.BlockSpec((D_FF, D_MODEL), const, pipeline_mode=single)]
    args = [x2d, o_gla, o_gdn, o_ssd, w_out, norm_w.reshape(1, D_MODEL), w_up, w_down]
    if last_layer:
        in_specs.append(pl.BlockSpec((1, D_MODEL), const))
        args.append(norm_f.reshape(1, D_MODEL))
    return pl.pallas_call(
        functools.partial(_outmlp_body, last_layer),
        grid=(t // tm,),
        in_specs=in_specs,
        out_specs=pl.BlockSpec((tm, D_MODEL), row),
        out_shape=jax.ShapeDtypeStruct((t, D_MODEL), f32),
        compiler_params=pltpu.CompilerParams(dimension_semantics=("arbitrary",), vmem_limit_bytes=VMEM_LIMIT),
        name="outmlp_final" if last_layer else "outmlp",
    )(*args)


def _prep_w_in(w):
    s = np.concatenate([[0], np.cumsum(IN_SPLITS)])
    seg = lambda n: w[:, int(s[n]):int(s[n + 1])]
    pad = jnp.zeros((w.shape[0], SMALL_W - 2 * GLA_RANK - 4 * GDN_HEADS - 2 * SSD_HEADS), w.dtype)
    cols = [seg(0), seg(1), seg(2), seg(3),
            seg(5), seg(6),
            seg(9), seg(10),
            seg(4), seg(7), seg(8), seg(11), pad]
    return jnp.concatenate(cols, axis=1).astype(bf16)


def _place_row(vals, off):
    return jnp.zeros((1, SMALL_W), f32).at[0, off:off + vals.shape[0]].set(vals.astype(f32))


def _expand_matrix(off, nheads, width):
    m = np.zeros((SMALL_W, nheads * width), np.float32)
    for h in range(nheads):
        m[off + h, h * width:(h + 1) * width] = 1.0
    return jnp.asarray(m, bf16)


def _cumsum_matrix(tb, rev):
    r = np.arange(tb)
    same = (r[:, None] // CHUNK) == (r[None, :] // CHUNK)
    tri = same & ((r[None, :] >= r[:, None]) if rev else (r[None, :] <= r[:, None]))
    return jnp.asarray(tri, bf16)


def _layer(x3, p):
    bsz, s, _ = x3.shape
    assert all(s % tb == 0 for tb in MIX_TB.values()) and s % PROJ_TM == 0, (bsz, s)
    x2d = x3.reshape(bsz * s, D_MODEL)
    gla_in, gdn_in, ssd_in, small = _inproj(
        x2d, s, p["norm_mix_w"], p["w_in_r"], p["gdn_conv_w"].astype(f32), p["ssd_conv_w"].astype(f32),
        p["ssd_conv_b"].astype(f32).reshape(1, SSD_CONV_CH))
    gla_in = gla_in.reshape(bsz, s, GLA_W)
    gdn_in = gdn_in.reshape(bsz, s, GDN_IN_W)
    ssd_in = ssd_in.reshape(bsz, s, SSD_IN_W)
    small = small.reshape(bsz, s, SMALL_W)

    outs = {}
    for name in ("gla", "gdn", "ssd"):
        prev = None
        for rev in (True, False):
            d = 1 if rev else 0
            final = not rev
            tri = _cumsum_matrix(MIX_TB[name], rev)
            if name == "gla":
                upw = jnp.zeros((SMALL_W, GLA_QK), f32).at[LR_OFF + d * GLA_RANK:LR_OFF + (d + 1) * GLA_RANK].set(
                    p["gla_gk_up"][d].astype(f32)).astype(bf16)
                upb = p["gla_gk_bias"][d].astype(f32).reshape(1, GLA_QK)
                nw = jnp.tile(p["gla_norm_w"].astype(f32), GLA_HEADS).reshape(1, GLA_V)
                prev = _mixer_call(_gla_body, "gla", gla_in, small, (upw, upb, tri), (prev, (nw,)), GLA_V,
                                   [pltpu.VMEM((GLA_DV, GLA_QK), f32)], rev, final)
            elif name == "gdn":
                params = (_place_row(p["gdn_A_log"][d], A_OFF + d * GDN_HEADS),
                          _place_row(p["gdn_dt_bias"][d], A_OFF + d * GDN_HEADS),
                          _expand_matrix(BETA_OFF + d * GDN_HEADS, GDN_HEADS, CHUNK),
                          _expand_matrix(A_OFF + d * GDN_HEADS, GDN_HEADS, CHUNK),
                          _expand_matrix(BETA_OFF + d * GDN_HEADS, GDN_HEADS, GDN_DH),
                          _expand_matrix(A_OFF + d * GDN_HEADS, GDN_HEADS, GDN_DH),
                          tri)
                nw = jnp.tile(p["gdn_norm_w"].astype(f32), GDN_HEADS).reshape(1, GDN_W)
                prev = _mixer_call(_gdn_body, "gdn", gdn_in, small, params, (prev, (nw,)), GDN_W,
                                   [pltpu.VMEM((GDN_HEADS, GDN_DH, GDN_DH), f32)], rev, final)
            else:
                params = (_place_row(p["ssd_A_log"][d], DT_OFF + d * SSD_HEADS),
                          _place_row(p["ssd_dt_bias"][d], DT_OFF + d * SSD_HEADS),
                          _expand_matrix(DT_OFF + d * SSD_HEADS, SSD_HEADS, SSD_HEADDIM),
                          tri)
                d_row = jnp.repeat(p["ssd_D"].astype(f32), SSD_HEADDIM).reshape(1, SSD_DINNER)
                nw = p["ssd_norm_w"].astype(f32).reshape(1, SSD_DINNER)
                prev = _mixer_call(_ssd_body, "ssd", ssd_in, small, params, (prev, (d_row, nw)), SSD_DINNER,
                                   [pltpu.VMEM((SSD_GROUPS, SSD_DSTATE, SSD_HPG * SSD_HEADDIM), f32)], rev, final)
        outs[name] = prev.reshape(bsz * s, -1)

    y = _outmlp(x2d, outs["gla"], outs["gdn"], outs["ssd"], p["w_out"].astype(bf16), p["norm_mlp_w"],
                p["w_up"].astype(bf16), p["w_down"].astype(bf16), p.get("norm_f_w"))
    return y.reshape(bsz, s, D_MODEL)


def kernel(x_prompt, x_sample, norm_mix_w, w_in, gla_gk_up, gla_gk_bias, gla_norm_w, gdn_conv_w, gdn_A_log, gdn_dt_bias, gdn_norm_w, ssd_conv_w, ssd_conv_b, ssd_A_log, ssd_dt_bias, ssd_D, ssd_norm_w, w_out, norm_mlp_w, w_up, w_down, norm_f_w):
    stacked = dict(norm_mix_w=norm_mix_w, gla_gk_up=gla_gk_up, gla_gk_bias=gla_gk_bias, gla_norm_w=gla_norm_w,
                   gdn_conv_w=gdn_conv_w, gdn_A_log=gdn_A_log, gdn_dt_bias=gdn_dt_bias, gdn_norm_w=gdn_norm_w,
                   ssd_conv_w=ssd_conv_w, ssd_conv_b=ssd_conv_b, ssd_A_log=ssd_A_log, ssd_dt_bias=ssd_dt_bias,
                   ssd_D=ssd_D, ssd_norm_w=ssd_norm_w, w_out=w_out, norm_mlp_w=norm_mlp_w, w_up=w_up,
                   w_down=w_down)
    depth = w_in.shape[0]
    layers = []
    for l in range(depth):
        p = {k: v[l] for k, v in stacked.items()}
        p["w_in_r"] = _prep_w_in(w_in[l])
        if l == depth - 1:
            p["norm_f_w"] = norm_f_w
        layers.append(p)

    def trunk(x):
        for p in layers:
            x = _layer(x, p)
        return x

    return (trunk(x_prompt), trunk(x_sample))
```

```python
import functools

import numpy as np
import jax
import jax.numpy as jnp
from jax import lax
from jax.experimental import pallas as pl
from jax.experimental.pallas import tpu as pltpu

f32 = jnp.float32
bf16 = jnp.bfloat16

D_MODEL = 1024
EPS = 1e-6
CHUNK = 64
CONV_K = 5
D_FF = 4 * D_MODEL
GLA_HEADS, GLA_DK, GLA_DV, GLA_RANK, GLA_NORMALIZER = 4, 64, 128, 16, 16.0
GDN_HEADS, GDN_DH = 4, 128
SSD_DINNER, SSD_HEADDIM, SSD_HEADS, SSD_GROUPS, SSD_DSTATE = 1024, 64, 16, 2, 64
SSD_HPG = SSD_HEADS // SSD_GROUPS
GLA_QK = GLA_HEADS * GLA_DK
GLA_V = GLA_HEADS * GLA_DV
GDN_W = GDN_HEADS * GDN_DH
SSD_BC = SSD_GROUPS * SSD_DSTATE
SSD_CONV_CH = SSD_DINNER + 2 * SSD_BC
D_MIX = GLA_V + GDN_W + SSD_DINNER
IN_SPLITS = (GLA_QK, GLA_QK, GLA_V, GLA_V, 2 * GLA_RANK,
             3 * GDN_W, GDN_W, 2 * GDN_HEADS, 2 * GDN_HEADS,
             SSD_DINNER, SSD_CONV_CH, 2 * SSD_HEADS)

SMALL_W = 128
LR_OFF, BETA_OFF, A_OFF, DT_OFF = 0, 32, 40, 48
GLA_W = 2 * GLA_QK + 2 * GLA_V
GDN_IN_W = 4 * GDN_W
SSD_IN_W = SSD_DINNER + SSD_CONV_CH
PROJ_W = GLA_W + GDN_IN_W + SSD_IN_W + SMALL_W

V7X_SUBLANES = 8
HALO = V7X_SUBLANES
VMEM_LIMIT = 56 * 1024 * 1024
MIX_TB = {"gla": 512, "gdn": 512, "ssd": 256}
PROJ_TM = 512
CONV_ROWS, CONV_COLS = 64, 256
PLAIN_COLS = 512
INV_BASE = 8
INV_BASE_PASSES = 3
FIN_ROWS = 32


def _dot(a, b):
    return jnp.dot(a, b, preferred_element_type=f32)


def _dot_t(a, b):
    return lax.dot_general(a, b, (((1,), (1,)), ((), ())), preferred_element_type=f32)


def _tdot(a, b):
    return lax.dot_general(a, b, (((0,), (0,)), ((), ())), preferred_element_type=f32)


def _split(x, n):
    parts, r = [], x
    for i in range(n):
        p = r.astype(bf16)
        parts.append(p)
        if i + 1 < n:
            r = r - p.astype(f32)
    return parts


def _dot_exact_left(m_bf, x, n=3):
    acc = None
    for p in _split(x, n):
        t = _dot(m_bf, p)
        acc = t if acc is None else acc + t
    return acc


def _expand(parts, e_bf):
    acc = None
    for p in parts:
        t = _dot(p, e_bf)
        acc = t if acc is None else acc + t
    return acc


def _silu(x):
    return x * jax.nn.sigmoid(x)


def _iota(shape, dim):
    return lax.broadcasted_iota(jnp.int32, shape, dim)


def _chunk_masks(width, rev):
    i = _iota((CHUNK, width), 0)
    j = _iota((CHUNK, width), 1) & (CHUNK - 1)
    if rev:
        return j >= i, j > i, j == i
    return j <= i, j < i, j == i


def _stack_heads(x, nheads, hw):
    lane = _iota(x.shape, 1)
    zero = jnp.zeros_like(x)
    return jnp.concatenate(
        [jnp.where((lane >= h * hw) & (lane < (h + 1) * hw), x, zero) for h in range(nheads)], axis=0)


def _chunk_order(nchunks, rev):
    return list(range(nchunks - 1, -1, -1)) if rev else list(range(nchunks))


def _scale_heads(x, nheads, hw, eps_scale_fn):
    outs = []
    for h in range(nheads):
        xh = x[:, h * hw:(h + 1) * hw]
        outs.append(xh * eps_scale_fn(jnp.sum(xh * xh, axis=-1, keepdims=True)))
    return jnp.concatenate(outs, axis=1)


def _l2norm_heads(x, nheads, hw, scale):
    return _scale_heads(x, nheads, hw, lambda ss: lax.rsqrt(ss + EPS) * scale)


def _rmsnorm_heads(x, nheads, hw):
    return _scale_heads(x, nheads, hw, lambda ss: lax.rsqrt(ss * (1.0 / hw) + EPS))


def _head_matmul(l, r, nheads, passes):
    r_hi = r.astype(bf16)
    bd_hi = _stack_heads(r_hi, nheads, CHUNK)
    l_hi = l.astype(bf16)
    out = _dot(l_hi, bd_hi)
    if passes >= 2:
        out = out + _dot((l - l_hi.astype(f32)).astype(bf16), bd_hi)
    if passes >= 3:
        out = out + _dot(l_hi, _stack_heads((r - r_hi.astype(f32)).astype(bf16), nheads, CHUNK))
    return out


def _unit_triangular_inverse(a_neg, chunks, eye, nheads):
    i = _iota(eye.shape, 0)
    j = _iota(eye.shape, 1) & (CHUNK - 1)
    same_block = lambda b: (i >> (b.bit_length() - 1)) == (j >> (b.bit_length() - 1))
    base = same_block(INV_BASE)
    pw, t = {}, {}
    for c in chunks:
        pw[c] = jnp.where(base, a_neg[c], 0.0)
        t[c] = eye + pw[c]
    for c in chunks:
        pw[c] = _head_matmul(pw[c], pw[c], nheads, INV_BASE_PASSES)
    n_factors = INV_BASE.bit_length() - 2
    for f in range(n_factors):
        for c in chunks:
            if f + 1 < n_factors:
                res = _head_matmul(jnp.concatenate([pw[c], t[c]], axis=0), pw[c], nheads, INV_BASE_PASSES)
                pw[c] = res[0:CHUNK]
                t[c] = t[c] + res[CHUNK:]
            else:
                t[c] = t[c] + _head_matmul(t[c], pw[c], nheads, INV_BASE_PASSES)
    b = INV_BASE
    while b < CHUNK:
        off = same_block(2 * b) & jnp.logical_not(same_block(b))
        z = {c: _head_matmul(jnp.where(off, a_neg[c], 0.0), t[c], nheads, 1) for c in chunks}
        for c in chunks:
            t[c] = t[c] + _head_matmul(t[c], z[c], nheads, 1)
        b *= 2
    return t


def _row_form(a_t, row0, nheads, cols):
    return jnp.concatenate([a_t[row0 + h:row0 + h + 1, cols] for h in range(nheads)], axis=1)


def _inproj_body(tiles_per_seq, x_ref, xp_ref, xn_ref, nw_ref, w_ref, gcw_ref, scw_ref, scb_ref,
                 gla_ref, gdn_ref, ssd_ref, small_ref, gbuf, sbuf):
    i = pl.program_id(0)
    tm = x_ref.shape[0]
    first = (i % tiles_per_seq) == 0
    last = (i % tiles_per_seq) == tiles_per_seq - 1
    norm = lambda v: v * lax.rsqrt(jnp.mean(v * v, axis=-1, keepdims=True) + EPS) * nw_ref[...]
    h_f32 = norm(x_ref[...])
    h_tile = h_f32.astype(bf16)
    h = jnp.concatenate([norm(xp_ref[...]), h_f32, norm(xn_ref[...])], axis=0).astype(bf16)

    c_gla, c_qkv = 0, GLA_W
    c_gate = c_qkv + 3 * GDN_W
    c_z = c_gate + GDN_W
    c_xbc = c_z + SSD_DINNER
    c_small = c_xbc + SSD_CONV_CH

    def plain(out_ref, out_col, w_col, width):
        out_ref[:, out_col:out_col + width] = _dot(h_tile, w_ref[:, w_col:w_col + width]).astype(out_ref.dtype)

    keep_prev = jnp.where(first, 0.0, 1.0)
    keep_next = jnp.where(last, 0.0, 1.0)

    def fill(buf, w_col):
        buf[...] = _dot(h, w_ref[:, w_col:w_col + buf.shape[1]])
        buf[0:HALO, :] = buf[0:HALO, :] * keep_prev
        buf[HALO + tm:, :] = buf[HALO + tm:, :] * keep_next

    def conv_tile(buf, cw_ref, b_ref, out_ref, out_col, r0, c0):
        cs = slice(c0, c0 + CONV_COLS)
        acc = None
        for j in range(CONV_K):
            t = buf[pl.ds(HALO - CONV_K // 2 + j + r0, CONV_ROWS), cs] * cw_ref[j:j + 1, cs]
            acc = t if acc is None else acc + t
        if b_ref is not None:
            acc = acc + b_ref[:, cs]
        out_ref[r0:r0 + CONV_ROWS, out_col + c0:out_col + c0 + CONV_COLS] = _silu(acc).astype(out_ref.dtype)

    fill(gbuf, c_qkv)
    fill(sbuf, c_xbc)
    conv_tiles = [(buf, cw, b, out, oc, r0, c0)
                  for buf, cw, b, out, oc in ((gbuf, gcw_ref, None, gdn_ref, 0),
                                              (sbuf, scw_ref, scb_ref, ssd_ref, SSD_DINNER))
                  for r0 in range(0, tm, CONV_ROWS) for c0 in range(0, buf.shape[1], CONV_COLS)]
    plains = [(out, oc + c0, wc + c0, min(PLAIN_COLS, width - c0))
              for out, oc, wc, width in ((gla_ref, 0, c_gla, GLA_W), (gdn_ref, 3 * GDN_W, c_gate, GDN_W),
                                         (ssd_ref, 0, c_z, SSD_DINNER), (small_ref, 0, c_small, SMALL_W))
              for c0 in range(0, width, PLAIN_COLS)]
    per = -(-len(conv_tiles) // len(plains))
    for n, args in enumerate(plains):
        plain(*args)
        for targs in conv_tiles[n * per:(n + 1) * per]:
            conv_tile(*targs)


def _inproj(x2d, seq_len, norm_w, w_r, gdn_conv_w, ssd_conv_w, ssd_conv_b):
    t = x2d.shape[0]
    tm = PROJ_TM
    nt = t // tm
    hb = tm // HALO
    widths = (GLA_W, GDN_IN_W, SSD_IN_W, SMALL_W)
    const = lambda i: (0, 0)
    return pl.pallas_call(
        functools.partial(_inproj_body, seq_len // tm),
        grid=(nt,),
        in_specs=[pl.BlockSpec((tm, D_MODEL), lambda i: (i, 0)),
                  pl.BlockSpec((HALO, D_MODEL), lambda i: (jnp.maximum(i * hb - 1, 0), 0)),
                  pl.BlockSpec((HALO, D_MODEL), lambda i: (jnp.minimum((i + 1) * hb, nt * hb - 1), 0)),
                  pl.BlockSpec((1, D_MODEL), const),
                  pl.BlockSpec((D_MODEL, PROJ_W), const, pipeline_mode=pl.Buffered(1)),
                  pl.BlockSpec((CONV_K, 3 * GDN_W), const),
                  pl.BlockSpec((CONV_K, SSD_CONV_CH), const),
                  pl.BlockSpec((1, SSD_CONV_CH), const)],
        out_specs=[pl.BlockSpec((tm, w), lambda i: (i, 0)) for w in widths],
        out_shape=[jax.ShapeDtypeStruct((t, w), f32 if w == SMALL_W else bf16) for w in widths],
        scratch_shapes=[pltpu.VMEM((tm + 2 * HALO, 3 * GDN_W), f32),
                        pltpu.VMEM((tm + 2 * HALO, SSD_CONV_CH), f32)],
        compiler_params=pltpu.CompilerParams(dimension_semantics=("arbitrary",), vmem_limit_bytes=VMEM_LIMIT),
        name="inproj",
    )(x2d, x2d, x2d, norm_w.reshape(1, D_MODEL), w_r, gdn_conv_w, ssd_conv_w, ssd_conv_b)


def _gla_body(rev, final, tb, *refs):
    if final:
        x_ref, small_ref, upw_ref, upb_ref, tri_ref, obwd_ref, nw_ref, out_ref, st_ref, o_buf = refs
    else:
        x_ref, small_ref, upw_ref, upb_ref, tri_ref, out_ref, st_ref = refs
        o_buf = None
    chunks = _chunk_order(tb // CHUNK, rev)
    rows = {c: slice(c * CHUNK, (c + 1) * CHUNK) for c in chunks}

    @pl.when(pl.program_id(1) == 0)
    def _():
        st_ref[...] = jnp.zeros_like(st_ref)

    q = x_ref[0, :, 0:GLA_QK].astype(f32) * (GLA_DK ** -0.5)
    k = x_ref[0, :, GLA_QK:2 * GLA_QK].astype(f32)
    v_bf = x_ref[0, :, 2 * GLA_QK:2 * GLA_QK + GLA_V].astype(bf16)
    logits = _dot(small_ref[0].astype(bf16), upw_ref[...]) + upb_ref[...]
    gk = jax.nn.log_sigmoid(logits) / GLA_NORMALIZER
    g_cum = _dot_exact_left(tri_ref[...], gk)
    q_in = (q * jnp.exp(g_cum)).astype(bf16)
    k_in = (k * jnp.exp(-g_cum)).astype(bf16)

    incl, _, _ = _chunk_masks(CHUNK, rev)
    att_mask = jnp.concatenate([incl] * GLA_HEADS, axis=0)
    lane = _iota((GLA_DV, GLA_QK), 1)
    qm, att, upd, g_tot = {}, {}, {}, {}
    for c in chunks:
        r = rows[c]
        last = c * CHUNK if rev else (c + 1) * CHUNK - 1
        g_tot[c] = g_cum[last:last + 1, :]
        qm[c] = _stack_heads(q_in[r], GLA_HEADS, GLA_DK)
        att[c] = jnp.where(att_mask, _dot_t(qm[c], k_in[r]), 0.0).astype(bf16)
        k_st = (k[r] * jnp.exp(g_tot[c] - g_cum[r])).astype(bf16)
        upd[c] = _tdot(v_bf[r], k_st)
    st = st_ref[...]
    for c in chunks:
        r = rows[c]
        q_st = _dot_t(qm[c], st.astype(bf16))
        for h in range(GLA_HEADS):
            hr = slice(h * CHUNK, (h + 1) * CHUNK)
            hv = slice(h * GLA_DV, (h + 1) * GLA_DV)
            o_h = _dot(att[c][hr], v_bf[r, hv]) + q_st[hr]
            if final:
                o_buf[r, hv] = o_h
            else:
                out_ref[0, r, hv] = o_h
        st = st * jnp.exp(g_tot[c])
        for h in range(GLA_HEADS):
            sel = (lane >= h * GLA_DK) & (lane < (h + 1) * GLA_DK)
            st = st + jnp.where(sel, upd[c][h * GLA_DV:(h + 1) * GLA_DV], 0.0)
    st_ref[...] = st

    if final:
        for r0 in range(0, tb, FIN_ROWS):
            rr = slice(r0, r0 + FIN_ROWS)
            tot = o_buf[rr, :] + obwd_ref[0, rr, :]
            gate = x_ref[0, rr, 2 * GLA_QK + GLA_V:].astype(f32)
            out_ref[0, rr, :] = (_rmsnorm_heads(tot, GLA_HEADS, GLA_DV) * nw_ref[...]
                                 * _silu(gate)).astype(out_ref.dtype)


def _mixer_index_maps(nb, rev):
    pos = (lambda i: nb - 1 - i) if rev else (lambda i: i)
    cur = lambda b, i: (b, pos(i), 0)
    const2 = lambda b, i: (0, 0)
    return cur, const2


def _mixer_call(body, name, x3, small3, params, extra_final, out_w, scratch, rev, final):
    bsz, s, in_w = x3.shape
    tb = MIX_TB[name]
    nb = s // tb
    cur, const2 = _mixer_index_maps(nb, rev)
    in_specs = [pl.BlockSpec((1, tb, in_w), cur), pl.BlockSpec((1, tb, SMALL_W), cur)]
    in_specs += [pl.BlockSpec(p.shape, const2) for p in params]
    args = [x3, small3, *params]
    scratch = list(scratch)
    if final:
        o_other, consts = extra_final
        in_specs += [pl.BlockSpec((1, tb, out_w), cur)] + [pl.BlockSpec(p.shape, const2) for p in consts]
        args += [o_other, *consts]
        scratch += [pltpu.VMEM((tb, out_w), f32)]
    return pl.pallas_call(
        functools.partial(body, rev, final, tb),
        grid=(bsz, nb),
        in_specs=in_specs,
        out_specs=pl.BlockSpec((1, tb, out_w), cur),
        out_shape=jax.ShapeDtypeStruct((bsz, s, out_w), bf16 if final else f32),
        scratch_shapes=scratch,
        compiler_params=pltpu.CompilerParams(dimension_semantics=("arbitrary", "arbitrary"),
                                             vmem_limit_bytes=VMEM_LIMIT),
        name=name + ("_fwd" if final else "_bwd"),
    )(*args)


def _gdn_body(rev, final, tb, *refs):
    (x_ref, small_ref, alog_ref, dtb_ref, e64b_ref, e64g_ref, e128b_ref, e128g_ref, tri_ref) = refs[:9]
    if final:
        obwd_ref, nw_ref, out_ref, st_ref, o_buf = refs[9:]
    else:
        out_ref, st_ref = refs[9:]
        o_buf = None
    nheads, dh = GDN_HEADS, GDN_DH
    chunks = _chunk_order(tb // CHUNK, rev)
    rows = {c: slice(c * CHUNK, (c + 1) * CHUNK) for c in chunks}
    heads = [slice(h * dh, (h + 1) * dh) for h in range(nheads)]
    hrows = [slice(h * CHUNK, (h + 1) * CHUNK) for h in range(nheads)]

    @pl.when(pl.program_id(1) == 0)
    def _():
        st_ref[...] = jnp.zeros_like(st_ref)

    q = _l2norm_heads(x_ref[0, :, 0:GDN_W].astype(f32), nheads, dh, dh ** -0.5)
    k = _l2norm_heads(x_ref[0, :, GDN_W:2 * GDN_W].astype(f32), nheads, dh, 1.0)
    v = x_ref[0, :, 2 * GDN_W:3 * GDN_W].astype(f32)
    k_bf = k.astype(bf16)
    q_bf = q.astype(bf16)

    sm = small_ref[0]
    beta_parts = _split(jax.nn.sigmoid(sm), 2)
    g_all = -jnp.exp(alog_ref[...]) * jax.nn.softplus(sm + dtb_ref[...])
    g_cum = _dot_exact_left(tri_ref[...], g_all)
    g_parts = _split(g_cum, 3)
    gc64 = _expand(g_parts, e64g_ref[...])
    gc128 = _expand(g_parts, e128g_ref[...])
    b64 = _expand(beta_parts, e64b_ref[...])
    b128 = _expand(beta_parts, e128b_ref[...])
    g_cum_t = g_cum.T
    g_row0 = A_OFF + (nheads if rev else 0)
    eg128 = jnp.exp(gc128)
    rhs_k = (k * b128 * eg128).astype(bf16)
    rhs_v = (v * b128).astype(bf16)
    q_dec = (q * eg128).astype(bf16)

    incl, strict, ident = _chunk_masks(nheads * CHUNK, rev)
    eye = jnp.where(ident, 1.0, 0.0)

    qk, kk = {}, {}
    for c in chunks:
        qk_l, kk_l = [], []
        for hs in heads:
            res = _dot_t(jnp.concatenate([q_bf[rows[c], hs], k_bf[rows[c], hs]], axis=0), k_bf[rows[c], hs])
            qk_l.append(res[0:CHUNK])
            kk_l.append(res[CHUNK:])
        qk[c] = jnp.concatenate(qk_l, axis=1)
        kk[c] = jnp.concatenate(kk_l, axis=1)
    a_qk, a_neg = {}, {}
    for c in chunks:
        g_row = _row_form(g_cum_t, g_row0, nheads, rows[c])
        decay = jnp.exp(jnp.where(incl, gc64[rows[c]] - g_row, -jnp.inf))
        a_qk[c] = _stack_heads((qk[c] * decay).astype(bf16), nheads, CHUNK)
        a_neg[c] = jnp.where(strict, -(b64[rows[c]] * kk[c] * decay), 0.0)
    t_inv = _unit_triangular_inverse(a_neg, chunks, eye, nheads)
    sol = {}
    for c in chunks:
        tm = _stack_heads(t_inv[c].astype(bf16), nheads, CHUNK)
        r_all = jnp.concatenate(
            [jnp.concatenate([rhs_k[rows[c], hs], rhs_v[rows[c], hs]], axis=1) for hs in heads], axis=0)
        sol[c] = _dot(tm, r_all)

    s = [st_ref[h] for h in range(nheads)]
    for c in chunks:
        r = rows[c]
        last = c * CHUNK if rev else (c + 1) * CHUNK - 1
        g_tot = gc128[last:last + 1, :]
        dec_u = jnp.exp(g_tot - gc128[r])
        e_tot = jnp.exp(g_tot)
        res = [_dot(jnp.concatenate([sol[c][hrows[h], 0:dh].astype(bf16), q_dec[r, heads[h]]], axis=0),
                    s[h].astype(bf16)) for h in range(nheads)]
        u = [sol[c][hrows[h], dh:] - res[h][0:CHUNK] for h in range(nheads)]
        upd = [_tdot(k_bf[r, heads[h]], (u[h] * dec_u[:, heads[h]]).astype(bf16)) for h in range(nheads)]
        s = [e_tot[:, heads[h]] * s[h] + upd[h] for h in range(nheads)]
        o_intra = _dot(a_qk[c], jnp.concatenate([x.astype(bf16) for x in u], axis=0))
        for h in range(nheads):
            o_h = o_intra[hrows[h]] + res[h][CHUNK:]
            if final:
                o_buf[r, heads[h]] = o_h
            else:
                out_ref[0, r, heads[h]] = o_h
    for h in range(nheads):
        st_ref[h] = s[h]

    if final:
        for r0 in range(0, tb, FIN_ROWS):
            rr = slice(r0, r0 + FIN_ROWS)
            tot = o_buf[rr, :] + obwd_ref[0, rr, :]
            gate = x_ref[0, rr, 3 * GDN_W:].astype(f32)
            out_ref[0, rr, :] = (_rmsnorm_heads(tot, nheads, dh) * nw_ref[...]
                                 * _silu(gate)).astype(out_ref.dtype)


def _ssd_body(rev, final, tb, *refs):
    (x_ref, small_ref, alog_ref, dtb_ref, e_ref, tri_ref) = refs[:6]
    if final:
        ybwd_ref, d_ref, nw_ref, out_ref, st_ref, y_buf = refs[6:]
    else:
        out_ref, st_ref = refs[6:]
        y_buf = None
    chunks = _chunk_order(tb // CHUNK, rev)
    rows = {c: slice(c * CHUNK, (c + 1) * CHUNK) for c in chunks}
    half = SSD_DINNER // SSD_GROUPS
    pair_w = 2 * SSD_HEADDIM
    pairs_per_group = SSD_HPG // 2

    @pl.when(pl.program_id(1) == 0)
    def _():
        st_ref[...] = jnp.zeros_like(st_ref)

    bm = x_ref[0, :, 2 * SSD_DINNER:2 * SSD_DINNER + SSD_BC].astype(bf16)
    cm = x_ref[0, :, 2 * SSD_DINNER + SSD_BC:].astype(bf16)

    dt_all = jax.nn.softplus(small_ref[0] + dtb_ref[...])
    la_all = dt_all * (-jnp.exp(alog_ref[...]))
    e_mat = e_ref[...]
    dt_e = _expand(_split(dt_all, 2), e_mat)
    a_cum = _dot_exact_left(tri_ref[...], la_all)
    a_col = _expand(_split(a_cum, 3), e_mat)
    a_cum_t = a_cum.T
    a_row0 = DT_OFF + (SSD_HEADS if rev else 0)
    incl, _, _ = _chunk_masks(pair_w, rev)
    lane = _iota((CHUNK, pair_w), 1)
    zero = jnp.zeros((CHUNK, pair_w), bf16)

    y_intra, d_state, e_tot = {}, {}, {}
    for c in chunks:
        r = rows[c]
        last = c * CHUNK if rev else (c + 1) * CHUNK - 1
        a_tot = a_col[last:last + 1, :]
        a_row = _row_form(a_cum_t, a_row0, SSD_HEADS, r)
        cb2 = []
        for g in range(SSD_GROUPS):
            gs = slice(g * SSD_DSTATE, (g + 1) * SSD_DSTATE)
            cb2.append(_dot_t(cm[r, gs], jnp.concatenate([bm[r, gs], bm[r, gs]], axis=0)))
        xw = []
        for pr in range(SSD_HEADS // 2):
            ps = slice(pr * pair_w, (pr + 1) * pair_w)
            a_cp = a_col[r, ps]
            xp = x_ref[0, r, SSD_DINNER + pr * pair_w:SSD_DINNER + (pr + 1) * pair_w].astype(f32) * dt_e[r, ps]
            xp_bf = xp.astype(bf16)
            decay = jnp.exp(jnp.where(incl, a_cp - a_row[:, ps], -jnp.inf))
            w_att = (decay * cb2[pr // pairs_per_group]).astype(bf16)
            bd = jnp.concatenate([jnp.where(lane < SSD_HEADDIM, xp_bf, zero),
                                  jnp.where(lane >= SSD_HEADDIM, xp_bf, zero)], axis=0)
            y_intra[c, pr] = _dot(w_att, bd)
            xw.append((xp * jnp.exp(a_tot[:, ps] - a_cp)).astype(bf16))
        e_tot[c] = jnp.exp(a_tot)
        d_state[c] = [_tdot(bm[r, g * SSD_DSTATE:(g + 1) * SSD_DSTATE],
                            jnp.concatenate(xw[g * pairs_per_group:(g + 1) * pairs_per_group], axis=1))
                      for g in range(SSD_GROUPS)]

    s = [st_ref[g] for g in range(SSD_GROUPS)]
    for c in chunks:
        r = rows[c]
        for g in range(SSD_GROUPS):
            gl = slice(g * half, (g + 1) * half)
            y_state = _dot(cm[r, g * SSD_DSTATE:(g + 1) * SSD_DSTATE], s[g].astype(bf16))
            s[g] = e_tot[c][:, gl] * s[g] + d_state[c][g]
            for pp in range(pairs_per_group):
                pr = g * pairs_per_group + pp
                ps = slice(pr * pair_w, (pr + 1) * pair_w)
                y_p = y_intra[c, pr] + y_state[:, pp * pair_w:(pp + 1) * pair_w] * jnp.exp(a_col[r, ps])
                if final:
                    y_buf[r, ps] = y_p
                else:
                    out_ref[0, r, ps] = y_p
    for g in range(SSD_GROUPS):
        st_ref[g] = s[g]

    if final:
        for r0 in range(0, tb, FIN_ROWS // 2):
            rr = slice(r0, r0 + FIN_ROWS // 2)
            y = (y_buf[rr, :] + ybwd_ref[0, rr, :]
                 + d_ref[...] * x_ref[0, rr, SSD_DINNER:2 * SSD_DINNER].astype(f32))
            yz = y * _silu(x_ref[0, rr, 0:SSD_DINNER].astype(f32))
            out_ref[0, rr, :] = (_rmsnorm_heads(yz, SSD_GROUPS, half) * nw_ref[...]).astype(out_ref.dtype)


def _outmlp_body(last_layer, *refs):
    if last_layer:
        x_ref, og_ref, od_ref, os_ref, wo_ref, nw_ref, wu_ref, wd_ref, nf_ref, out_ref = refs
    else:
        x_ref, og_ref, od_ref, os_ref, wo_ref, nw_ref, wu_ref, wd_ref, out_ref = refs
    mix = (_dot(og_ref[...], wo_ref[0:GLA_V, :])
           + _dot(od_ref[...], wo_ref[GLA_V:GLA_V + GDN_W, :])
           + _dot(os_ref[...], wo_ref[GLA_V + GDN_W:, :]))
    x1 = x_ref[...] + mix
    h = (x1 * lax.rsqrt(jnp.mean(x1 * x1, axis=-1, keepdims=True) + EPS) * nw_ref[...]).astype(bf16)
    mlp = None
    ff_blk = D_FF // 4
    for j in range(4):
        up = jnp.maximum(_dot(h, wu_ref[:, j * ff_blk:(j + 1) * ff_blk]), 0.0)
        t = _dot((up * up).astype(bf16), wd_ref[j * ff_blk:(j + 1) * ff_blk, :])
        mlp = t if mlp is None else mlp + t
    acc = x1 + mlp
    if last_layer:
        acc = acc * lax.rsqrt(jnp.mean(acc * acc, axis=-1, keepdims=True) + EPS) * nf_ref[...]
    out_ref[...] = acc


def _outmlp(x2d, o_gla, o_gdn, o_ssd, w_out, norm_w, w_up, w_down, norm_f):
    t = x2d.shape[0]
    tm = PROJ_TM
    last_layer = norm_f is not None
    row = lambda i: (i, 0)
    const = lambda i: (0, 0)
    single = pl.Buffered(1)
    in_specs = [pl.BlockSpec((tm, D_MODEL), row),
                pl.BlockSpec((tm, GLA_V), row),
                pl.BlockSpec((tm, GDN_W), row),
                pl.BlockSpec((tm, SSD_DINNER), row),
                pl.BlockSpec((D_MIX, D_MODEL), const, pipeline_mode=single),
                pl.BlockSpec((1, D_MODEL), const),
                pl.BlockSpec((D_MODEL, D_FF), const, pipeline_mode=single),
                pl.BlockSpec((D_FF, D_MODEL), const, pipeline_mode=single)]
    args = [x2d, o_gla, o_gdn, o_ssd, w_out, norm_w.reshape(1, D_MODEL), w_up, w_down]
    if last_layer:
        in_specs.append(pl.BlockSpec((1, D_MODEL), const))
        args.append(norm_f.reshape(1, D_MODEL))
    return pl.pallas_call(
        functools.partial(_outmlp_body, last_layer),
        grid=(t // tm,),
        in_specs=in_specs,
        out_specs=pl.BlockSpec((tm, D_MODEL), row),
        out_shape=jax.ShapeDtypeStruct((t, D_MODEL), f32),
        compiler_params=pltpu.CompilerParams(dimension_semantics=("arbitrary",), vmem_limit_bytes=VMEM_LIMIT),
        name="outmlp_final" if last_layer else "outmlp",
    )(*args)


def _prep_w_in(w):
    s = np.concatenate([[0], np.cumsum(IN_SPLITS)])
    seg = lambda n: w[:, int(s[n]):int(s[n + 1])]
    pad = jnp.zeros((w.shape[0], SMALL_W - 2 * GLA_RANK - 4 * GDN_HEADS - 2 * SSD_HEADS), w.dtype)
    cols = [seg(0), seg(1), seg(2), seg(3),
            seg(5), seg(6),
            seg(9), seg(10),
            seg(4), seg(7), seg(8), seg(11), pad]
    return jnp.concatenate(cols, axis=1).astype(bf16)


def _place_row(vals, off):
    return jnp.zeros((1, SMALL_W), f32).at[0, off:off + vals.shape[0]].set(vals.astype(f32))


def _expand_matrix(off, nheads, width):
    m = np.zeros((SMALL_W, nheads * width), np.float32)
    for h in range(nheads):
        m[off + h, h * width:(h + 1) * width] = 1.0
    return jnp.asarray(m, bf16)


def _cumsum_matrix(tb, rev):
    r = np.arange(tb)
    same = (r[:, None] // CHUNK) == (r[None, :] // CHUNK)
    tri = same & ((r[None, :] >= r[:, None]) if rev else (r[None, :] <= r[:, None]))
    return jnp.asarray(tri, bf16)


def _layer(x3, p):
    bsz, s, _ = x3.shape
    assert all(s % tb == 0 for tb in MIX_TB.values()) and s % PROJ_TM == 0, (bsz, s)
    x2d = x3.reshape(bsz * s, D_MODEL)
    gla_in, gdn_in, ssd_in, small = _inproj(
        x2d, s, p["norm_mix_w"], p["w_in_r"], p["gdn_conv_w"].astype(f32), p["ssd_conv_w"].astype(f32),
        p["ssd_conv_b"].astype(f32).reshape(1, SSD_CONV_CH))
    gla_in = gla_in.reshape(bsz, s, GLA_W)
    gdn_in = gdn_in.reshape(bsz, s, GDN_IN_W)
    ssd_in = ssd_in.reshape(bsz, s, SSD_IN_W)
    small = small.reshape(bsz, s, SMALL_W)

    outs = {}
    for name in ("gla", "gdn", "ssd"):
        prev = None
        for rev in (True, False):
            d = 1 if rev else 0
            final = not rev
            tri = _cumsum_matrix(MIX_TB[name], rev)
            if name == "gla":
                upw = jnp.zeros((SMALL_W, GLA_QK), f32).at[LR_OFF + d * GLA_RANK:LR_OFF + (d + 1) * GLA_RANK].set(
                    p["gla_gk_up"][d].astype(f32)).astype(bf16)
                upb = p["gla_gk_bias"][d].astype(f32).reshape(1, GLA_QK)
                nw = jnp.tile(p["gla_norm_w"].astype(f32), GLA_HEADS).reshape(1, GLA_V)
                prev = _mixer_call(_gla_body, "gla", gla_in, small, (upw, upb, tri), (prev, (nw,)), GLA_V,
                                   [pltpu.VMEM((GLA_DV, GLA_QK), f32)], rev, final)
            elif name == "gdn":
                params = (_place_row(p["gdn_A_log"][d], A_OFF + d * GDN_HEADS),
                          _place_row(p["gdn_dt_bias"][d], A_OFF + d * GDN_HEADS),
                          _expand_matrix(BETA_OFF + d * GDN_HEADS, GDN_HEADS, CHUNK),
                          _expand_matrix(A_OFF + d * GDN_HEADS, GDN_HEADS, CHUNK),
                          _expand_matrix(BETA_OFF + d * GDN_HEADS, GDN_HEADS, GDN_DH),
                          _expand_matrix(A_OFF + d * GDN_HEADS, GDN_HEADS, GDN_DH),
                          tri)
                nw = jnp.tile(p["gdn_norm_w"].astype(f32), GDN_HEADS).reshape(1, GDN_W)
                prev = _mixer_call(_gdn_body, "gdn", gdn_in, small, params, (prev, (nw,)), GDN_W,
                                   [pltpu.VMEM((GDN_HEADS, GDN_DH, GDN_DH), f32)], rev, final)
            else:
                params = (_place_row(p["ssd_A_log"][d], DT_OFF + d * SSD_HEADS),
                          _place_row(p["ssd_dt_bias"][d], DT_OFF + d * SSD_HEADS),
                          _expand_matrix(DT_OFF + d * SSD_HEADS, SSD_HEADS, SSD_HEADDIM),
                          tri)
                d_row = jnp.repeat(p["ssd_D"].astype(f32), SSD_HEADDIM).reshape(1, SSD_DINNER)
                nw = p["ssd_norm_w"].astype(f32).reshape(1, SSD_DINNER)
                prev = _mixer_call(_ssd_body, "ssd", ssd_in, small, params, (prev, (d_row, nw)), SSD_DINNER,
                                   [pltpu.VMEM((SSD_GROUPS, SSD_DSTATE, SSD_HPG * SSD_HEADDIM), f32)], rev, final)
        outs[name] = prev.reshape(bsz * s, -1)

    y = _outmlp(x2d, outs["gla"], outs["gdn"], outs["ssd"], p["w_out"].astype(bf16), p["norm_mlp_w"],
                p["w_up"].astype(bf16), p["w_down"].astype(bf16), p.get("norm_f_w"))
    return y.reshape(bsz, s, D_MODEL)


def kernel(x_prompt, x_sample, norm_mix_w, w_in, gla_gk_up, gla_gk_bias, gla_norm_w, gdn_conv_w, gdn_A_log, gdn_dt_bias, gdn_norm_w, ssd_conv_w, ssd_conv_b, ssd_A_log, ssd_dt_bias, ssd_D, ssd_norm_w, w_out, norm_mlp_w, w_up, w_down, norm_f_w):
    stacked = dict(norm_mix_w=norm_mix_w, gla_gk_up=gla_gk_up, gla_gk_bias=gla_gk_bias, gla_norm_w=gla_norm_w,
                   gdn_conv_w=gdn_conv_w, gdn_A_log=gdn_A_log, gdn_dt_bias=gdn_dt_bias, gdn_norm_w=gdn_norm_w,
                   ssd_conv_w=ssd_conv_w, ssd_conv_b=ssd_conv_b, ssd_A_log=ssd_A_log, ssd_dt_bias=ssd_dt_bias,
                   ssd_D=ssd_D, ssd_norm_w=ssd_norm_w, w_out=w_out, norm_mlp_w=norm_mlp_w, w_up=w_up,
                   w_down=w_down)
    depth = w_in.shape[0]
    layers = []
    for l in range(depth):
        p = {k: v[l] for k, v in stacked.items()}
        p["w_in_r"] = _prep_w_in(w_in[l])
        if l == depth - 1:
            p["norm_f_w"] = norm_f_w
        layers.append(p)

    def trunk(x):
        for p in layers:
            x = _layer(x, p)
        return x

    return (trunk(x_prompt), trunk(x_sample))
```

```python
import functools

import numpy as np
import jax
import jax.numpy as jnp
from jax import lax
from jax.experimental import pallas as pl
from jax.experimental.pallas import tpu as pltpu

f32 = jnp.float32
bf16 = jnp.bfloat16

D_MODEL = 1024
EPS = 1e-6
CHUNK = 64
CONV_K = 5
D_FF = 4 * D_MODEL
GLA_HEADS, GLA_DK, GLA_DV, GLA_RANK, GLA_NORMALIZER = 4, 64, 128, 16, 16.0
GDN_HEADS, GDN_DH = 4, 128
SSD_DINNER, SSD_HEADDIM, SSD_HEADS, SSD_GROUPS, SSD_DSTATE = 1024, 64, 16, 2, 64
SSD_HPG = SSD_HEADS // SSD_GROUPS
GLA_QK = GLA_HEADS * GLA_DK
GLA_V = GLA_HEADS * GLA_DV
GDN_W = GDN_HEADS * GDN_DH
SSD_BC = SSD_GROUPS * SSD_DSTATE
SSD_CONV_CH = SSD_DINNER + 2 * SSD_BC
D_MIX = GLA_V + GDN_W + SSD_DINNER
IN_SPLITS = (GLA_QK, GLA_QK, GLA_V, GLA_V, 2 * GLA_RANK,
             3 * GDN_W, GDN_W, 2 * GDN_HEADS, 2 * GDN_HEADS,
             SSD_DINNER, SSD_CONV_CH, 2 * SSD_HEADS)

SMALL_W = 128
LR_OFF, BETA_OFF, A_OFF, DT_OFF = 0, 32, 40, 48
GLA_W = 2 * GLA_QK + 2 * GLA_V
GDN_IN_W = 4 * GDN_W
SSD_IN_W = SSD_DINNER + SSD_CONV_CH
PROJ_W = GLA_W + GDN_IN_W + SSD_IN_W + SMALL_W

V7X_SUBLANES = 8
HALO = V7X_SUBLANES
VMEM_LIMIT = 56 * 1024 * 1024
MIX_TB = {"gla": 512, "gdn": 512, "ssd": 256}
PROJ_TM = 512
CONV_ROWS, CONV_COLS = 64, 256
PLAIN_COLS = 512
INV_BASE = 8
INV_BASE_PASSES = 2
FIN_ROWS = 32


def _dot(a, b):
    return jnp.dot(a, b, preferred_element_type=f32)


def _dot_t(a, b):
    return lax.dot_general(a, b, (((1,), (1,)), ((), ())), preferred_element_type=f32)


def _tdot(a, b):
    return lax.dot_general(a, b, (((0,), (0,)), ((), ())), preferred_element_type=f32)


def _split(x, n):
    parts, r = [], x
    for i in range(n):
        p = r.astype(bf16)
        parts.append(p)
        if i + 1 < n:
            r = r - p.astype(f32)
    return parts


def _dot_exact_left(m_bf, x, n=3):
    w = x.shape[1]
    res = _dot(m_bf, jnp.concatenate(_split(x, n), axis=1))
    acc = res[:, 0:w]
    for k in range(1, n):
        acc = acc + res[:, k * w:(k + 1) * w]
    return acc


def _expand(parts, e_bf):
    m = parts[0].shape[0]
    res = _dot(jnp.concatenate(parts, axis=0), e_bf)
    acc = res[0:m]
    for n in range(1, len(parts)):
        acc = acc + res[n * m:(n + 1) * m]
    return acc


def _silu(x):
    return x * jax.nn.sigmoid(x)


def _iota(shape, dim):
    return lax.broadcasted_iota(jnp.int32, shape, dim)


def _chunk_masks(width, rev):
    i = _iota((CHUNK, width), 0)
    j = _iota((CHUNK, width), 1) & (CHUNK - 1)
    if rev:
        return j >= i, j > i, j == i
    return j <= i, j < i, j == i


def _stack_heads(x, nheads, hw):
    lane = _iota(x.shape, 1)
    zero = jnp.zeros_like(x)
    return jnp.concatenate(
        [jnp.where((lane >= h * hw) & (lane < (h + 1) * hw), x, zero) for h in range(nheads)], axis=0)


def _chunk_order(nchunks, rev):
    return list(range(nchunks - 1, -1, -1)) if rev else list(range(nchunks))


def _scale_heads(x, nheads, hw, eps_scale_fn):
    outs = []
    for h in range(nheads):
        xh = x[:, h * hw:(h + 1) * hw]
        outs.append(xh * eps_scale_fn(jnp.sum(xh * xh, axis=-1, keepdims=True)))
    return jnp.concatenate(outs, axis=1)


def _l2norm_heads(x, nheads, hw, scale):
    return _scale_heads(x, nheads, hw, lambda ss: lax.rsqrt(ss + EPS) * scale)


def _rmsnorm_heads(x, nheads, hw):
    return _scale_heads(x, nheads, hw, lambda ss: lax.rsqrt(ss * (1.0 / hw) + EPS))


def _head_matmul(l, r, nheads, passes):
    r_hi = r.astype(bf16)
    bd_hi = _stack_heads(r_hi, nheads, CHUNK)
    l_hi = l.astype(bf16)
    if passes >= 2:
        m = l.shape[0]
        both = _dot(jnp.concatenate([l_hi, (l - l_hi.astype(f32)).astype(bf16)], axis=0), bd_hi)
        out = both[0:m] + both[m:]
    else:
        out = _dot(l_hi, bd_hi)
    if passes >= 3:
        out = out + _dot(l_hi, _stack_heads((r - r_hi.astype(f32)).astype(bf16), nheads, CHUNK))
    return out


def _unit_triangular_inverse(a_neg, chunks, eye, nheads):
    i = _iota(eye.shape, 0)
    j = _iota(eye.shape, 1) & (CHUNK - 1)
    same_block = lambda b: (i >> (b.bit_length() - 1)) == (j >> (b.bit_length() - 1))
    base = same_block(INV_BASE)
    pw, t = {}, {}
    for c in chunks:
        pw[c] = jnp.where(base, a_neg[c], 0.0)
        t[c] = eye + pw[c]
    for c in chunks:
        pw[c] = _head_matmul(pw[c], pw[c], nheads, INV_BASE_PASSES)
    n_factors = INV_BASE.bit_length() - 2
    for f in range(n_factors):
        for c in chunks:
            if f + 1 < n_factors:
                res = _head_matmul(jnp.concatenate([pw[c], t[c]], axis=0), pw[c], nheads, INV_BASE_PASSES)
                pw[c] = res[0:CHUNK]
                t[c] = t[c] + res[CHUNK:]
            else:
                t[c] = t[c] + _head_matmul(t[c], pw[c], nheads, INV_BASE_PASSES)
    b = INV_BASE
    while b < CHUNK:
        off = same_block(2 * b) & jnp.logical_not(same_block(b))
        z = {c: _head_matmul(jnp.where(off, a_neg[c], 0.0), t[c], nheads, 1) for c in chunks}
        for c in chunks:
            t[c] = t[c] + _head_matmul(t[c], z[c], nheads, 1)
        b *= 2
    return t


def _row_form(a_t, row0, nheads, cols):
    return jnp.concatenate([a_t[row0 + h:row0 + h + 1, cols] for h in range(nheads)], axis=1)


def _inproj_body(tiles_per_seq, x_ref, xp_ref, xn_ref, nw_ref, w_ref, gcw_ref, scw_ref, scb_ref,
                 gla_ref, gdn_ref, ssd_ref, small_ref, gbuf, sbuf):
    i = pl.program_id(0)
    tm = x_ref.shape[0]
    first = (i % tiles_per_seq) == 0
    last = (i % tiles_per_seq) == tiles_per_seq - 1
    norm = lambda v: v * lax.rsqrt(jnp.mean(v * v, axis=-1, keepdims=True) + EPS) * nw_ref[...]
    h_f32 = norm(x_ref[...])
    h_tile = h_f32.astype(bf16)
    h = jnp.concatenate([norm(xp_ref[...]), h_f32, norm(xn_ref[...])], axis=0).astype(bf16)

    c_gla, c_qkv = 0, GLA_W
    c_gate = c_qkv + 3 * GDN_W
    c_z = c_gate + GDN_W
    c_xbc = c_z + SSD_DINNER
    c_small = c_xbc + SSD_CONV_CH

    def plain(out_ref, out_col, w_col, width):
        out_ref[:, out_col:out_col + width] = _dot(h_tile, w_ref[:, w_col:w_col + width]).astype(out_ref.dtype)

    keep_prev = jnp.where(first, 0.0, 1.0)
    keep_next = jnp.where(last, 0.0, 1.0)

    def fill(buf, w_col):
        buf[...] = _dot(h, w_ref[:, w_col:w_col + buf.shape[1]])
        buf[0:HALO, :] = buf[0:HALO, :] * keep_prev
        buf[HALO + tm:, :] = buf[HALO + tm:, :] * keep_next

    def conv_tile(buf, cw_ref, b_ref, out_ref, out_col, r0, c0):
        cs = slice(c0, c0 + CONV_COLS)
        acc = None
        for j in range(CONV_K):
            t = buf[pl.ds(HALO - CONV_K // 2 + j + r0, CONV_ROWS), cs] * cw_ref[j:j + 1, cs]
            acc = t if acc is None else acc + t
        if b_ref is not None:
            acc = acc + b_ref[:, cs]
        out_ref[r0:r0 + CONV_ROWS, out_col + c0:out_col + c0 + CONV_COLS] = _silu(acc).astype(out_ref.dtype)

    fill(gbuf, c_qkv)
    fill(sbuf, c_xbc)
    conv_tiles = [(buf, cw, b, out, oc, r0, c0)
                  for buf, cw, b, out, oc in ((gbuf, gcw_ref, None, gdn_ref, 0),
                                              (sbuf, scw_ref, scb_ref, ssd_ref, SSD_DINNER))
                  for r0 in range(0, tm, CONV_ROWS) for c0 in range(0, buf.shape[1], CONV_COLS)]
    plains = [(out, oc + c0, wc + c0, min(PLAIN_COLS, width - c0))
              for out, oc, wc, width in ((gla_ref, 0, c_gla, GLA_W), (gdn_ref, 3 * GDN_W, c_gate, GDN_W),
                                         (ssd_ref, 0, c_z, SSD_DINNER), (small_ref, 0, c_small, SMALL_W))
              for c0 in range(0, width, PLAIN_COLS)]
    per = -(-len(conv_tiles) // len(plains))
    for n, args in enumerate(plains):
        plain(*args)
        for targs in conv_tiles[n * per:(n + 1) * per]:
            conv_tile(*targs)


def _inproj(x2d, seq_len, norm_w, w_r, gdn_conv_w, ssd_conv_w, ssd_conv_b):
    t = x2d.shape[0]
    tm = PROJ_TM
    nt = t // tm
    hb = tm // HALO
    widths = (GLA_W, GDN_IN_W, SSD_IN_W, SMALL_W)
    const = lambda i: (0, 0)
    return pl.pallas_call(
        functools.partial(_inproj_body, seq_len // tm),
        grid=(nt,),
        in_specs=[pl.BlockSpec((tm, D_MODEL), lambda i: (i, 0)),
                  pl.BlockSpec((HALO, D_MODEL), lambda i: (jnp.maximum(i * hb - 1, 0), 0)),
                  pl.BlockSpec((HALO, D_MODEL), lambda i: (jnp.minimum((i + 1) * hb, nt * hb - 1), 0)),
                  pl.BlockSpec((1, D_MODEL), const),
                  pl.BlockSpec((D_MODEL, PROJ_W), const, pipeline_mode=pl.Buffered(1)),
                  pl.BlockSpec((CONV_K, 3 * GDN_W), const),
                  pl.BlockSpec((CONV_K, SSD_CONV_CH), const),
                  pl.BlockSpec((1, SSD_CONV_CH), const)],
        out_specs=[pl.BlockSpec((tm, w), lambda i: (i, 0)) for w in widths],
        out_shape=[jax.ShapeDtypeStruct((t, w), f32 if w == SMALL_W else bf16) for w in widths],
        scratch_shapes=[pltpu.VMEM((tm + 2 * HALO, 3 * GDN_W), f32),
                        pltpu.VMEM((tm + 2 * HALO, SSD_CONV_CH), f32)],
        compiler_params=pltpu.CompilerParams(dimension_semantics=("arbitrary",), vmem_limit_bytes=VMEM_LIMIT),
        name="inproj",
    )(x2d, x2d, x2d, norm_w.reshape(1, D_MODEL), w_r, gdn_conv_w, ssd_conv_w, ssd_conv_b)


def _gla_body(rev, final, tb, *refs):
    if final:
        x_ref, small_ref, upw_ref, upb_ref, tri_ref, obwd_ref, nw_ref, out_ref, st_ref, o_buf = refs
    else:
        x_ref, small_ref, upw_ref, upb_ref, tri_ref, out_ref, st_ref = refs
        o_buf = None
    chunks = _chunk_order(tb // CHUNK, rev)
    rows = {c: slice(c * CHUNK, (c + 1) * CHUNK) for c in chunks}

    @pl.when(pl.program_id(1) == 0)
    def _():
        st_ref[...] = jnp.zeros_like(st_ref)

    q = x_ref[0, :, 0:GLA_QK].astype(f32) * (GLA_DK ** -0.5)
    k = x_ref[0, :, GLA_QK:2 * GLA_QK].astype(f32)
    v_bf = x_ref[0, :, 2 * GLA_QK:2 * GLA_QK + GLA_V].astype(bf16)
    logits = _dot(small_ref[0].astype(bf16), upw_ref[...]) + upb_ref[...]
    gk = jax.nn.log_sigmoid(logits) / GLA_NORMALIZER
    g_cum = _dot_exact_left(tri_ref[...], gk)
    q_in = (q * jnp.exp(g_cum)).astype(bf16)
    k_in = (k * jnp.exp(-g_cum)).astype(bf16)

    incl, _, _ = _chunk_masks(CHUNK, rev)
    att_mask = jnp.concatenate([incl] * GLA_HEADS, axis=0)
    lane = _iota((GLA_DV, GLA_QK), 1)
    qm, att, upd, g_tot = {}, {}, {}, {}
    for c in chunks:
        r = rows[c]
        last = c * CHUNK if rev else (c + 1) * CHUNK - 1
        g_tot[c] = g_cum[last:last + 1, :]
        qm[c] = _stack_heads(q_in[r], GLA_HEADS, GLA_DK)
        att[c] = jnp.where(att_mask, _dot_t(qm[c], k_in[r]), 0.0).astype(bf16)
        k_st = (k[r] * jnp.exp(g_tot[c] - g_cum[r])).astype(bf16)
        upd[c] = _tdot(v_bf[r], k_st)
    st = st_ref[...]
    for c in chunks:
        r = rows[c]
        q_st = _dot_t(qm[c], st.astype(bf16))
        for h in range(GLA_HEADS):
            hr = slice(h * CHUNK, (h + 1) * CHUNK)
            hv = slice(h * GLA_DV, (h + 1) * GLA_DV)
            o_h = _dot(att[c][hr], v_bf[r, hv]) + q_st[hr]
            if final:
                o_buf[r, hv] = o_h
            else:
                out_ref[0, r, hv] = o_h
        st = st * jnp.exp(g_tot[c])
        for h in range(GLA_HEADS):
            sel = (lane >= h * GLA_DK) & (lane < (h + 1) * GLA_DK)
            st = st + jnp.where(sel, upd[c][h * GLA_DV:(h + 1) * GLA_DV], 0.0)
    st_ref[...] = st

    if final:
        for r0 in range(0, tb, FIN_ROWS):
            rr = slice(r0, r0 + FIN_ROWS)
            tot = o_buf[rr, :] + obwd_ref[0, rr, :]
            gate = x_ref[0, rr, 2 * GLA_QK + GLA_V:].astype(f32)
            out_ref[0, rr, :] = (_rmsnorm_heads(tot, GLA_HEADS, GLA_DV) * nw_ref[...]
                                 * _silu(gate)).astype(out_ref.dtype)


def _mixer_index_maps(nb, rev):
    pos = (lambda i: nb - 1 - i) if rev else (lambda i: i)
    cur = lambda b, i: (b, pos(i), 0)
    const2 = lambda b, i: (0, 0)
    return cur, const2


def _mixer_call(body, name, x3, small3, params, extra_final, out_w, scratch, rev, final):
    bsz, s, in_w = x3.shape
    tb = MIX_TB[name]
    nb = s // tb
    cur, const2 = _mixer_index_maps(nb, rev)
    in_specs = [pl.BlockSpec((1, tb, in_w), cur), pl.BlockSpec((1, tb, SMALL_W), cur)]
    in_specs += [pl.BlockSpec(p.shape, const2) for p in params]
    args = [x3, small3, *params]
    scratch = list(scratch)
    if final:
        o_other, consts = extra_final
        in_specs += [pl.BlockSpec((1, tb, out_w), cur)] + [pl.BlockSpec(p.shape, const2) for p in consts]
        args += [o_other, *consts]
        scratch += [pltpu.VMEM((tb, out_w), f32)]
    return pl.pallas_call(
        functools.partial(body, rev, final, tb),
        grid=(bsz, nb),
        in_specs=in_specs,
        out_specs=pl.BlockSpec((1, tb, out_w), cur),
        out_shape=jax.ShapeDtypeStruct((bsz, s, out_w), bf16 if final else f32),
        scratch_shapes=scratch,
        compiler_params=pltpu.CompilerParams(dimension_semantics=("arbitrary", "arbitrary"),
                                             vmem_limit_bytes=VMEM_LIMIT),
        name=name + ("_fwd" if final else "_bwd"),
    )(*args)


def _gdn_body(rev, final, tb, *refs):
    (x_ref, small_ref, alog_ref, dtb_ref, e64b_ref, e64g_ref, e128b_ref, e128g_ref, tri_ref) = refs[:9]
    if final:
        obwd_ref, nw_ref, out_ref, st_ref, o_buf = refs[9:]
    else:
        out_ref, st_ref = refs[9:]
        o_buf = None
    nheads, dh = GDN_HEADS, GDN_DH
    chunks = _chunk_order(tb // CHUNK, rev)
    rows = {c: slice(c * CHUNK, (c + 1) * CHUNK) for c in chunks}
    heads = [slice(h * dh, (h + 1) * dh) for h in range(nheads)]
    hrows = [slice(h * CHUNK, (h + 1) * CHUNK) for h in range(nheads)]

    @pl.when(pl.program_id(1) == 0)
    def _():
        st_ref[...] = jnp.zeros_like(st_ref)

    q = _l2norm_heads(x_ref[0, :, 0:GDN_W].astype(f32), nheads, dh, dh ** -0.5)
    k = _l2norm_heads(x_ref[0, :, GDN_W:2 * GDN_W].astype(f32), nheads, dh, 1.0)
    v = x_ref[0, :, 2 * GDN_W:3 * GDN_W].astype(f32)
    k_bf = k.astype(bf16)
    q_bf = q.astype(bf16)

    sm = small_ref[0]
    beta_parts = _split(jax.nn.sigmoid(sm), 2)
    g_all = -jnp.exp(alog_ref[...]) * jax.nn.softplus(sm + dtb_ref[...])
    g_cum = _dot_exact_left(tri_ref[...], g_all)
    g_parts = _split(g_cum, 3)
    gc64 = _expand(g_parts, e64g_ref[...])
    gc128 = _expand(g_parts, e128g_ref[...])
    b64 = _expand(beta_parts, e64b_ref[...])
    b128 = _expand(beta_parts, e128b_ref[...])
    g_cum_t = g_cum.T
    g_row0 = A_OFF + (nheads if rev else 0)
    eg128 = jnp.exp(gc128)
    rhs_k = (k * b128 * eg128).astype(bf16)
    rhs_v = (v * b128).astype(bf16)
    q_dec = (q * eg128).astype(bf16)

    incl, strict, ident = _chunk_masks(nheads * CHUNK, rev)
    eye = jnp.where(ident, 1.0, 0.0)

    qk, kk = {}, {}
    for c in chunks:
        qk_l, kk_l = [], []
        for hs in heads:
            res = _dot_t(jnp.concatenate([q_bf[rows[c], hs], k_bf[rows[c], hs]], axis=0), k_bf[rows[c], hs])
            qk_l.append(res[0:CHUNK])
            kk_l.append(res[CHUNK:])
        qk[c] = jnp.concatenate(qk_l, axis=1)
        kk[c] = jnp.concatenate(kk_l, axis=1)
    a_qk, a_neg = {}, {}
    for c in chunks:
        g_row = _row_form(g_cum_t, g_row0, nheads, rows[c])
        decay = jnp.exp(jnp.where(incl, gc64[rows[c]] - g_row, -jnp.inf))
        a_qk[c] = _stack_heads((qk[c] * decay).astype(bf16), nheads, CHUNK)
        a_neg[c] = jnp.where(strict, -(b64[rows[c]] * kk[c] * decay), 0.0)
    t_inv = _unit_triangular_inverse(a_neg, chunks, eye, nheads)
    sol = {}
    for c in chunks:
        tm = _stack_heads(t_inv[c].astype(bf16), nheads, CHUNK)
        r_all = jnp.concatenate(
            [jnp.concatenate([rhs_k[rows[c], hs], rhs_v[rows[c], hs]], axis=1) for hs in heads], axis=0)
        sol[c] = _dot(tm, r_all)

    s = [st_ref[h] for h in range(nheads)]
    for c in chunks:
        r = rows[c]
        last = c * CHUNK if rev else (c + 1) * CHUNK - 1
        g_tot = gc128[last:last + 1, :]
        dec_u = jnp.exp(g_tot - gc128[r])
        e_tot = jnp.exp(g_tot)
        res = [_dot(jnp.concatenate([sol[c][hrows[h], 0:dh].astype(bf16), q_dec[r, heads[h]]], axis=0),
                    s[h].astype(bf16)) for h in range(nheads)]
        u = [sol[c][hrows[h], dh:] - res[h][0:CHUNK] for h in range(nheads)]
        upd = [_tdot(k_bf[r, heads[h]], (u[h] * dec_u[:, heads[h]]).astype(bf16)) for h in range(nheads)]
        s = [e_tot[:, heads[h]] * s[h] + upd[h] for h in range(nheads)]
        o_intra = _dot(a_qk[c], jnp.concatenate([x.astype(bf16) for x in u], axis=0))
        for h in range(nheads):
            o_h = o_intra[hrows[h]] + res[h][CHUNK:]
            if final:
                o_buf[r, heads[h]] = o_h
            else:
                out_ref[0, r, heads[h]] = o_h
    for h in range(nheads):
        st_ref[h] = s[h]

    if final:
        for r0 in range(0, tb, FIN_ROWS):
            rr = slice(r0, r0 + FIN_ROWS)
            tot = o_buf[rr, :] + obwd_ref[0, rr, :]
            gate = x_ref[0, rr, 3 * GDN_W:].astype(f32)
            out_ref[0, rr, :] = (_rmsnorm_heads(tot, nheads, dh) * nw_ref[...]
                                 * _silu(gate)).astype(out_ref.dtype)


def _ssd_body(rev, final, tb, *refs):
    (x_ref, small_ref, alog_ref, dtb_ref, e_ref, tri_ref) = refs[:6]
    if final:
        ybwd_ref, d_ref, nw_ref, out_ref, st_ref, y_buf = refs[6:]
    else:
        out_ref, st_ref = refs[6:]
        y_buf = None
    chunks = _chunk_order(tb // CHUNK, rev)
    rows = {c: slice(c * CHUNK, (c + 1) * CHUNK) for c in chunks}
    half = SSD_DINNER // SSD_GROUPS
    pair_w = 2 * SSD_HEADDIM
    pairs_per_group = SSD_HPG // 2

    @pl.when(pl.program_id(1) == 0)
    def _():
        st_ref[...] = jnp.zeros_like(st_ref)

    bm = x_ref[0, :, 2 * SSD_DINNER:2 * SSD_DINNER + SSD_BC].astype(bf16)
    cm = x_ref[0, :, 2 * SSD_DINNER + SSD_BC:].astype(bf16)

    dt_all = jax.nn.softplus(small_ref[0] + dtb_ref[...])
    la_all = dt_all * (-jnp.exp(alog_ref[...]))
    e_mat = e_ref[...]
    dt_e = _expand(_split(dt_all, 2), e_mat)
    a_cum = _dot_exact_left(tri_ref[...], la_all)
    a_col = _expand(_split(a_cum, 3), e_mat)
    a_cum_t = a_cum.T
    a_row0 = DT_OFF + (SSD_HEADS if rev else 0)
    incl, _, _ = _chunk_masks(pair_w, rev)
    lane = _iota((CHUNK, pair_w), 1)
    zero = jnp.zeros((CHUNK, pair_w), bf16)

    y_intra, d_state, e_tot = {}, {}, {}
    for c in chunks:
        r = rows[c]
        last = c * CHUNK if rev else (c + 1) * CHUNK - 1
        a_tot = a_col[last:last + 1, :]
        a_row = _row_form(a_cum_t, a_row0, SSD_HEADS, r)
        cb2 = []
        for g in range(SSD_GROUPS):
            gs = slice(g * SSD_DSTATE, (g + 1) * SSD_DSTATE)
            cb2.append(_dot_t(cm[r, gs], jnp.concatenate([bm[r, gs], bm[r, gs]], axis=0)))
        xw = []
        for pr in range(SSD_HEADS // 2):
            ps = slice(pr * pair_w, (pr + 1) * pair_w)
            a_cp = a_col[r, ps]
            xp = x_ref[0, r, SSD_DINNER + pr * pair_w:SSD_DINNER + (pr + 1) * pair_w].astype(f32) * dt_e[r, ps]
            xp_bf = xp.astype(bf16)
            decay = jnp.exp(jnp.where(incl, a_cp - a_row[:, ps], -jnp.inf))
            w_att = (decay * cb2[pr // pairs_per_group]).astype(bf16)
            bd = jnp.concatenate([jnp.where(lane < SSD_HEADDIM, xp_bf, zero),
                                  jnp.where(lane >= SSD_HEADDIM, xp_bf, zero)], axis=0)
            y_intra[c, pr] = _dot(w_att, bd)
            xw.append((xp * jnp.exp(a_tot[:, ps] - a_cp)).astype(bf16))
        e_tot[c] = jnp.exp(a_tot)
        d_state[c] = [_tdot(bm[r, g * SSD_DSTATE:(g + 1) * SSD_DSTATE],
                            jnp.concatenate(xw[g * pairs_per_group:(g + 1) * pairs_per_group], axis=1))
                      for g in range(SSD_GROUPS)]

    s = [st_ref[g] for g in range(SSD_GROUPS)]
    for c in chunks:
        r = rows[c]
        for g in range(SSD_GROUPS):
            gl = slice(g * half, (g + 1) * half)
            y_state = _dot(cm[r, g * SSD_DSTATE:(g + 1) * SSD_DSTATE], s[g].astype(bf16))
            s[g] = e_tot[c][:, gl] * s[g] + d_state[c][g]
            for pp in range(pairs_per_group):
                pr = g * pairs_per_group + pp
                ps = slice(pr * pair_w, (pr + 1) * pair_w)
                y_p = y_intra[c, pr] + y_state[:, pp * pair_w:(pp + 1) * pair_w] * jnp.exp(a_col[r, ps])
                if final:
                    y_buf[r, ps] = y_p
                else:
                    out_ref[0, r, ps] = y_p
    for g in range(SSD_GROUPS):
        st_ref[g] = s[g]

    if final:
        for r0 in range(0, tb, FIN_ROWS // 2):
            rr = slice(r0, r0 + FIN_ROWS // 2)
            y = (y_buf[rr, :] + ybwd_ref[0, rr, :]
                 + d_ref[...] * x_ref[0, rr, SSD_DINNER:2 * SSD_DINNER].astype(f32))
            yz = y * _silu(x_ref[0, rr, 0:SSD_DINNER].astype(f32))
            out_ref[0, rr, :] = (_rmsnorm_heads(yz, SSD_GROUPS, half) * nw_ref[...]).astype(out_ref.dtype)


def _outmlp_body(last_layer, *refs):
    if last_layer:
        x_ref, og_ref, od_ref, os_ref, wo_ref, nw_ref, wu_ref, wd_ref, nf_ref, out_ref = refs
    else:
        x_ref, og_ref, od_ref, os_ref, wo_ref, nw_ref, wu_ref, wd_ref, out_ref = refs
    mix = (_dot(og_ref[...], wo_ref[0:GLA_V, :])
           + _dot(od_ref[...], wo_ref[GLA_V:GLA_V + GDN_W, :])
           + _dot(os_ref[...], wo_ref[GLA_V + GDN_W:, :]))
    x1 = x_ref[...] + mix
    h = (x1 * lax.rsqrt(jnp.mean(x1 * x1, axis=-1, keepdims=True) + EPS) * nw_ref[...]).astype(bf16)
    mlp = None
    ff_blk = D_FF // 4
    for j in range(4):
        up = jnp.maximum(_dot(h, wu_ref[:, j * ff_blk:(j + 1) * ff_blk]), 0.0)
        t = _dot((up * up).astype(bf16), wd_ref[j * ff_blk:(j + 1) * ff_blk, :])
        mlp = t if mlp is None else mlp + t
    acc = x1 + mlp
    if last_layer:
        acc = acc * lax.rsqrt(jnp.mean(acc * acc, axis=-1, keepdims=True) + EPS) * nf_ref[...]
    out_ref[...] = acc


def _outmlp(x2d, o_gla, o_gdn, o_ssd, w_out, norm_w, w_up, w_down, norm_f):
    t = x2d.shape[0]
    tm = PROJ_TM
    last_layer = norm_f is not None
    row = lambda i: (i, 0)
    const = lambda i: (0, 0)
    single = pl.Buffered(1)
    in_specs = [pl.BlockSpec((tm, D_MODEL), row),
                pl.BlockSpec((tm, GLA_V), row),
                pl.BlockSpec((tm, GDN_W), row),
                pl.BlockSpec((tm, SSD_DINNER), row),
                pl.BlockSpec((D_MIX, D_MODEL), const, pipeline_mode=single),
                pl.BlockSpec((1, D_MODEL), const),
                pl.BlockSpec((D_MODEL, D_FF), const, pipeline_mode=single),
                pl.BlockSpec((D_FF, D_MODEL), const, pipeline_mode=single)]
    args = [x2d, o_gla, o_gdn, o_ssd, w_out, norm_w.reshape(1, D_MODEL), w_up, w_down]
    if last_layer:
        in_specs.append(pl.BlockSpec((1, D_MODEL), const))
        args.append(norm_f.reshape(1, D_MODEL))
    return pl.pallas_call(
        functools.partial(_outmlp_body, last_layer),
        grid=(t // tm,),
        in_specs=in_specs,
        out_specs=pl.BlockSpec((tm, D_MODEL), row),
        out_shape=jax.ShapeDtypeStruct((t, D_MODEL), f32),
        compiler_params=pltpu.CompilerParams(dimension_semantics=("arbitrary",), vmem_limit_bytes=VMEM_LIMIT),
        name="outmlp_final" if last_layer else "outmlp",
    )(*args)


def _prep_w_in(w):
    s = np.concatenate([[0], np.cumsum(IN_SPLITS)])
    seg = lambda n: w[:, int(s[n]):int(s[n + 1])]
    pad = jnp.zeros((w.shape[0], SMALL_W - 2 * GLA_RANK - 4 * GDN_HEADS - 2 * SSD_HEADS), w.dtype)
    cols = [seg(0), seg(1), seg(2), seg(3),
            seg(5), seg(6),
            seg(9), seg(10),
            seg(4), seg(7), seg(8), seg(11), pad]
    return jnp.concatenate(cols, axis=1).astype(bf16)


def _place_row(vals, off):
    return jnp.zeros((1, SMALL_W), f32).at[0, off:off + vals.shape[0]].set(vals.astype(f32))


def _expand_matrix(off, nheads, width):
    m = np.zeros((SMALL_W, nheads * width), np.float32)
    for h in range(nheads):
        m[off + h, h * width:(h + 1) * width] = 1.0
    return jnp.asarray(m, bf16)


def _cumsum_matrix(tb, rev):
    r = np.arange(tb)
    same = (r[:, None] // CHUNK) == (r[None, :] // CHUNK)
    tri = same & ((r[None, :] >= r[:, None]) if rev else (r[None, :] <= r[:, None]))
    return jnp.asarray(tri, bf16)


def _layer(x3, p):
    bsz, s, _ = x3.shape
    assert all(s % tb == 0 for tb in MIX_TB.values()) and s % PROJ_TM == 0, (bsz, s)
    x2d = x3.reshape(bsz * s, D_MODEL)
    gla_in, gdn_in, ssd_in, small = _inproj(
        x2d, s, p["norm_mix_w"], p["w_in_r"], p["gdn_conv_w"].astype(f32), p["ssd_conv_w"].astype(f32),
        p["ssd_conv_b"].astype(f32).reshape(1, SSD_CONV_CH))
    gla_in = gla_in.reshape(bsz, s, GLA_W)
    gdn_in = gdn_in.reshape(bsz, s, GDN_IN_W)
    ssd_in = ssd_in.reshape(bsz, s, SSD_IN_W)
    small = small.reshape(bsz, s, SMALL_W)

    outs = {}
    for name in ("gla", "gdn", "ssd"):
        prev = None
        for rev in (True, False):
            d = 1 if rev else 0
            final = not rev
            tri = _cumsum_matrix(MIX_TB[name], rev)
            if name == "gla":
                upw = jnp.zeros((SMALL_W, GLA_QK), f32).at[LR_OFF + d * GLA_RANK:LR_OFF + (d + 1) * GLA_RANK].set(
                    p["gla_gk_up"][d].astype(f32)).astype(bf16)
                upb = p["gla_gk_bias"][d].astype(f32).reshape(1, GLA_QK)
                nw = jnp.tile(p["gla_norm_w"].astype(f32), GLA_HEADS).reshape(1, GLA_V)
                prev = _mixer_call(_gla_body, "gla", gla_in, small, (upw, upb, tri), (prev, (nw,)), GLA_V,
                                   [pltpu.VMEM((GLA_DV, GLA_QK), f32)], rev, final)
            elif name == "gdn":
                params = (_place_row(p["gdn_A_log"][d], A_OFF + d * GDN_HEADS),
                          _place_row(p["gdn_dt_bias"][d], A_OFF + d * GDN_HEADS),
                          _expand_matrix(BETA_OFF + d * GDN_HEADS, GDN_HEADS, CHUNK),
                          _expand_matrix(A_OFF + d * GDN_HEADS, GDN_HEADS, CHUNK),
                          _expand_matrix(BETA_OFF + d * GDN_HEADS, GDN_HEADS, GDN_DH),
                          _expand_matrix(A_OFF + d * GDN_HEADS, GDN_HEADS, GDN_DH),
                          tri)
                nw = jnp.tile(p["gdn_norm_w"].astype(f32), GDN_HEADS).reshape(1, GDN_W)
                prev = _mixer_call(_gdn_body, "gdn", gdn_in, small, params, (prev, (nw,)), GDN_W,
                                   [pltpu.VMEM((GDN_HEADS, GDN_DH, GDN_DH), f32)], rev, final)
            else:
                params = (_place_row(p["ssd_A_log"][d], DT_OFF + d * SSD_HEADS),
                          _place_row(p["ssd_dt_bias"][d], DT_OFF + d * SSD_HEADS),
                          _expand_matrix(DT_OFF + d * SSD_HEADS, SSD_HEADS, SSD_HEADDIM),
                          tri)
                d_row = jnp.repeat(p["ssd_D"].astype(f32), SSD_HEADDIM).reshape(1, SSD_DINNER)
                nw = p["ssd_norm_w"].astype(f32).reshape(1, SSD_DINNER)
                prev = _mixer_call(_ssd_body, "ssd", ssd_in, small, params, (prev, (d_row, nw)), SSD_DINNER,
                                   [pltpu.VMEM((SSD_GROUPS, SSD_DSTATE, SSD_HPG * SSD_HEADDIM), f32)], rev, final)
        outs[name] = prev.reshape(bsz * s, -1)

    y = _outmlp(x2d, outs["gla"], outs["gdn"], outs["ssd"], p["w_out"].astype(bf16), p["norm_mlp_w"],
                p["w_up"].astype(bf16), p["w_down"].astype(bf16), p.get("norm_f_w"))
    return y.reshape(bsz, s, D_MODEL)


def kernel(x_prompt, x_sample, norm_mix_w, w_in, gla_gk_up, gla_gk_bias, gla_norm_w, gdn_conv_w, gdn_A_log, gdn_dt_bias, gdn_norm_w, ssd_conv_w, ssd_conv_b, ssd_A_log, ssd_dt_bias, ssd_D, ssd_norm_w, w_out, norm_mlp_w, w_up, w_down, norm_f_w):
    stacked = dict(norm_mix_w=norm_mix_w, gla_gk_up=gla_gk_up, gla_gk_bias=gla_gk_bias, gla_norm_w=gla_norm_w,
                   gdn_conv_w=gdn_conv_w, gdn_A_log=gdn_A_log, gdn_dt_bias=gdn_dt_bias, gdn_norm_w=gdn_norm_w,
                   ssd_conv_w=ssd_conv_w, ssd_conv_b=ssd_conv_b, ssd_A_log=ssd_A_log, ssd_dt_bias=ssd_dt_bias,
                   ssd_D=ssd_D, ssd_norm_w=ssd_norm_w, w_out=w_out, norm_mlp_w=norm_mlp_w, w_up=w_up,
                   w_down=w_down)
    depth = w_in.shape[0]
    layers = []
    for l in range(depth):
        p = {k: v[l] for k, v in stacked.items()}
        p["w_in_r"] = _prep_w_in(w_in[l])
        if l == depth - 1:
            p["norm_f_w"] = norm_f_w
        layers.append(p)

    def trunk(x):
        for p in layers:
            x = _layer(x, p)
        return x

    return (trunk(x_prompt), trunk(x_sample))
```

```python
import functools

import numpy as np
import jax
import jax.numpy as jnp
from jax import lax
from jax.experimental import pallas as pl
from jax.experimental.pallas import tpu as pltpu

f32 = jnp.float32
bf16 = jnp.bfloat16

D_MODEL = 1024
EPS = 1e-6
CHUNK = 64
CONV_K = 5
D_FF = 4 * D_MODEL
GLA_HEADS, GLA_DK, GLA_DV, GLA_RANK, GLA_NORMALIZER = 4, 64, 128, 16, 16.0
GDN_HEADS, GDN_DH = 4, 128
SSD_DINNER, SSD_HEADDIM, SSD_HEADS, SSD_GROUPS, SSD_DSTATE = 1024, 64, 16, 2, 64
SSD_HPG = SSD_HEADS // SSD_GROUPS
GLA_QK = GLA_HEADS * GLA_DK
GLA_V = GLA_HEADS * GLA_DV
GDN_W = GDN_HEADS * GDN_DH
SSD_BC = SSD_GROUPS * SSD_DSTATE
SSD_CONV_CH = SSD_DINNER + 2 * SSD_BC
D_MIX = GLA_V + GDN_W + SSD_DINNER
IN_SPLITS = (GLA_QK, GLA_QK, GLA_V, GLA_V, 2 * GLA_RANK,
             3 * GDN_W, GDN_W, 2 * GDN_HEADS, 2 * GDN_HEADS,
             SSD_DINNER, SSD_CONV_CH, 2 * SSD_HEADS)

SMALL_W = 128
LR_OFF, BETA_OFF, A_OFF, DT_OFF = 0, 32, 40, 48
GLA_W = 2 * GLA_QK + 2 * GLA_V
GDN_IN_W = 4 * GDN_W
SSD_IN_W = SSD_DINNER + SSD_CONV_CH
PROJ_W = GLA_W + GDN_IN_W + SSD_IN_W + SMALL_W

V7X_SUBLANES = 8
HALO = V7X_SUBLANES
VMEM_LIMIT = 56 * 1024 * 1024
MIX_TB = {"gla": 512, "gdn": 512, "ssd": 512}
CUM_ROWS = 128
PROJ_TM = 512
CONV_ROWS, CONV_COLS = 64, 256
INV_BASE = 8
INV_BASE_PASSES = 2
FIN_ROWS = 32


def _dot(a, b):
    return jnp.dot(a, b, preferred_element_type=f32)


def _dot_t(a, b):
    return lax.dot_general(a, b, (((1,), (1,)), ((), ())), preferred_element_type=f32)


def _tdot(a, b):
    return lax.dot_general(a, b, (((0,), (0,)), ((), ())), preferred_element_type=f32)


def _split(x, n):
    parts, r = [], x
    for i in range(n):
        p = r.astype(bf16)
        parts.append(p)
        if i + 1 < n:
            r = r - p.astype(f32)
    return parts


def _dot_exact_left(m_bf, x, n=3):
    w = x.shape[1]
    res = _dot(m_bf, jnp.concatenate(_split(x, n), axis=1))
    acc = res[:, 0:w]
    for k in range(1, n):
        acc = acc + res[:, k * w:(k + 1) * w]
    return acc


def _chunk_cumsum(tri_ref, x):
    rows = tri_ref.shape[0]
    return jnp.concatenate([_dot_exact_left(tri_ref[...], x[r0:r0 + rows])
                            for r0 in range(0, x.shape[0], rows)], axis=0)


def _expand(parts, e_bf):
    m = parts[0].shape[0]
    res = _dot(jnp.concatenate(parts, axis=0), e_bf)
    acc = res[0:m]
    for n in range(1, len(parts)):
        acc = acc + res[n * m:(n + 1) * m]
    return acc


def _silu(x):
    return x * jax.nn.sigmoid(x)


def _iota(shape, dim):
    return lax.broadcasted_iota(jnp.int32, shape, dim)


def _chunk_masks(width, rev):
    i = _iota((CHUNK, width), 0)
    j = _iota((CHUNK, width), 1) & (CHUNK - 1)
    if rev:
        return j >= i, j > i, j == i
    return j <= i, j < i, j == i


def _stack_heads(x, nheads, hw):
    lane = _iota(x.shape, 1)
    zero = jnp.zeros_like(x)
    return jnp.concatenate(
        [jnp.where((lane >= h * hw) & (lane < (h + 1) * hw), x, zero) for h in range(nheads)], axis=0)


def _chunk_order(nchunks, rev):
    return list(range(nchunks - 1, -1, -1)) if rev else list(range(nchunks))


def _scale_heads(x, nheads, hw, eps_scale_fn):
    outs = []
    for h in range(nheads):
        xh = x[:, h * hw:(h + 1) * hw]
        outs.append(xh * eps_scale_fn(jnp.sum(xh * xh, axis=-1, keepdims=True)))
    return jnp.concatenate(outs, axis=1)


def _l2norm_heads(x, nheads, hw, scale):
    return _scale_heads(x, nheads, hw, lambda ss: lax.rsqrt(ss + EPS) * scale)


def _rmsnorm_heads(x, nheads, hw):
    return _scale_heads(x, nheads, hw, lambda ss: lax.rsqrt(ss * (1.0 / hw) + EPS))


def _head_matmul(l, r, nheads, passes):
    r_hi = r.astype(bf16)
    bd_hi = _stack_heads(r_hi, nheads, CHUNK)
    l_hi = l.astype(bf16)
    if passes >= 2:
        m = l.shape[0]
        both = _dot(jnp.concatenate([l_hi, (l - l_hi.astype(f32)).astype(bf16)], axis=0), bd_hi)
        out = both[0:m] + both[m:]
    else:
        out = _dot(l_hi, bd_hi)
    if passes >= 3:
        out = out + _dot(l_hi, _stack_heads((r - r_hi.astype(f32)).astype(bf16), nheads, CHUNK))
    return out


def _unit_triangular_inverse(a_neg, chunks, eye, nheads):
    i = _iota(eye.shape, 0)
    j = _iota(eye.shape, 1) & (CHUNK - 1)
    same_block = lambda b: (i >> (b.bit_length() - 1)) == (j >> (b.bit_length() - 1))
    base = same_block(INV_BASE)
    pw, t = {}, {}
    for c in chunks:
        pw[c] = jnp.where(base, a_neg[c], 0.0)
        t[c] = eye + pw[c]
    for c in chunks:
        pw[c] = _head_matmul(pw[c], pw[c], nheads, INV_BASE_PASSES)
    n_factors = INV_BASE.bit_length() - 2
    for f in range(n_factors):
        for c in chunks:
            if f + 1 < n_factors:
                res = _head_matmul(jnp.concatenate([pw[c], t[c]], axis=0), pw[c], nheads, INV_BASE_PASSES)
                pw[c] = res[0:CHUNK]
                t[c] = t[c] + res[CHUNK:]
            else:
                t[c] = t[c] + _head_matmul(t[c], pw[c], nheads, INV_BASE_PASSES)
    b = INV_BASE
    while b < CHUNK:
        off = same_block(2 * b) & jnp.logical_not(same_block(b))
        z = {c: _head_matmul(jnp.where(off, a_neg[c], 0.0), t[c], nheads, 1) for c in chunks}
        for c in chunks:
            t[c] = t[c] + _head_matmul(t[c], z[c], nheads, 1)
        b *= 2
    return t


def _row_form(a_t, row0, nheads, cols):
    return jnp.concatenate([a_t[row0 + h:row0 + h + 1, cols] for h in range(nheads)], axis=1)


def _inproj_body(tiles_per_seq, x_ref, xp_ref, xn_ref, nw_ref, w_ref, gcw_ref, scw_ref, scb_ref,
                 gla_ref, gdn_ref, ssd_ref, small_ref, gbuf, sbuf):
    i = pl.program_id(0)
    tm = x_ref.shape[0]
    first = (i % tiles_per_seq) == 0
    last = (i % tiles_per_seq) == tiles_per_seq - 1
    norm = lambda v: v * lax.rsqrt(jnp.mean(v * v, axis=-1, keepdims=True) + EPS) * nw_ref[...]
    h_f32 = norm(x_ref[...])
    h_tile = h_f32.astype(bf16)
    h = jnp.concatenate([norm(xp_ref[...]), h_f32, norm(xn_ref[...])], axis=0).astype(bf16)

    c_gla, c_qkv = 0, GLA_W
    c_gate = c_qkv + 3 * GDN_W
    c_z = c_gate + GDN_W
    c_xbc = c_z + SSD_DINNER
    c_small = c_xbc + SSD_CONV_CH

    def plain(out_ref, out_col, w_col, width):
        out_ref[:, out_col:out_col + width] = _dot(h_tile, w_ref[:, w_col:w_col + width]).astype(out_ref.dtype)

    keep_prev = jnp.where(first, 0.0, 1.0)
    keep_next = jnp.where(last, 0.0, 1.0)

    def fill(buf, w_col):
        buf[...] = _dot(h, w_ref[:, w_col:w_col + buf.shape[1]])
        buf[0:HALO, :] = buf[0:HALO, :] * keep_prev
        buf[HALO + tm:, :] = buf[HALO + tm:, :] * keep_next

    def conv_tile(buf, cw_ref, b_ref, out_ref, out_col, r0, c0):
        cs = slice(c0, c0 + CONV_COLS)
        acc = None
        for j in range(CONV_K):
            t = buf[pl.ds(HALO - CONV_K // 2 + j + r0, CONV_ROWS), cs] * cw_ref[j:j + 1, cs]
            acc = t if acc is None else acc + t
        if b_ref is not None:
            acc = acc + b_ref[:, cs]
        out_ref[r0:r0 + CONV_ROWS, out_col + c0:out_col + c0 + CONV_COLS] = _silu(acc).astype(out_ref.dtype)

    fill(gbuf, c_qkv)
    fill(sbuf, c_xbc)
    plain(gla_ref, 0, c_gla, GLA_W)
    plain(gdn_ref, 3 * GDN_W, c_gate, GDN_W)
    plain(ssd_ref, 0, c_z, SSD_DINNER)
    plain(small_ref, 0, c_small, SMALL_W)
    for buf, cw, b, out, oc in ((gbuf, gcw_ref, None, gdn_ref, 0), (sbuf, scw_ref, scb_ref, ssd_ref, SSD_DINNER)):
        for r0 in range(0, tm, CONV_ROWS):
            for c0 in range(0, buf.shape[1], CONV_COLS):
                conv_tile(buf, cw, b, out, oc, r0, c0)


def _inproj(x2d, seq_len, norm_w, w_r, gdn_conv_w, ssd_conv_w, ssd_conv_b):
    t = x2d.shape[0]
    tm = PROJ_TM
    nt = t // tm
    hb = tm // HALO
    widths = (GLA_W, GDN_IN_W, SSD_IN_W, SMALL_W)
    const = lambda i: (0, 0)
    return pl.pallas_call(
        functools.partial(_inproj_body, seq_len // tm),
        grid=(nt,),
        in_specs=[pl.BlockSpec((tm, D_MODEL), lambda i: (i, 0)),
                  pl.BlockSpec((HALO, D_MODEL), lambda i: (jnp.maximum(i * hb - 1, 0), 0)),
                  pl.BlockSpec((HALO, D_MODEL), lambda i: (jnp.minimum((i + 1) * hb, nt * hb - 1), 0)),
                  pl.BlockSpec((1, D_MODEL), const),
                  pl.BlockSpec((D_MODEL, PROJ_W), const, pipeline_mode=pl.Buffered(1)),
                  pl.BlockSpec((CONV_K, 3 * GDN_W), const),
                  pl.BlockSpec((CONV_K, SSD_CONV_CH), const),
                  pl.BlockSpec((1, SSD_CONV_CH), const)],
        out_specs=[pl.BlockSpec((tm, w), lambda i: (i, 0)) for w in widths],
        out_shape=[jax.ShapeDtypeStruct((t, w), f32 if w == SMALL_W else bf16) for w in widths],
        scratch_shapes=[pltpu.VMEM((tm + 2 * HALO, 3 * GDN_W), f32),
                        pltpu.VMEM((tm + 2 * HALO, SSD_CONV_CH), f32)],
        compiler_params=pltpu.CompilerParams(dimension_semantics=("arbitrary",), vmem_limit_bytes=VMEM_LIMIT),
        name="inproj",
    )(x2d, x2d, x2d, norm_w.reshape(1, D_MODEL), w_r, gdn_conv_w, ssd_conv_w, ssd_conv_b)


def _gla_body(rev, final, tb, *refs):
    if final:
        x_ref, small_ref, upw_ref, upb_ref, tri_ref, obwd_ref, nw_ref, out_ref, st_ref, o_buf = refs
    else:
        x_ref, small_ref, upw_ref, upb_ref, tri_ref, out_ref, st_ref = refs
        o_buf = None
    chunks = _chunk_order(tb // CHUNK, rev)
    rows = {c: slice(c * CHUNK, (c + 1) * CHUNK) for c in chunks}

    @pl.when(pl.program_id(1) == 0)
    def _():
        st_ref[...] = jnp.zeros_like(st_ref)

    q = x_ref[0, :, 0:GLA_QK].astype(f32) * (GLA_DK ** -0.5)
    k = x_ref[0, :, GLA_QK:2 * GLA_QK].astype(f32)
    v_bf = x_ref[0, :, 2 * GLA_QK:2 * GLA_QK + GLA_V].astype(bf16)
    logits = _dot(small_ref[0].astype(bf16), upw_ref[...]) + upb_ref[...]
    gk = jax.nn.log_sigmoid(logits) / GLA_NORMALIZER
    g_cum = _chunk_cumsum(tri_ref, gk)
    q_in = (q * jnp.exp(g_cum)).astype(bf16)
    k_in = (k * jnp.exp(-g_cum)).astype(bf16)

    incl, _, _ = _chunk_masks(CHUNK, rev)
    att_mask = jnp.concatenate([incl] * GLA_HEADS, axis=0)
    lane = _iota((GLA_DV, GLA_QK), 1)
    qm, att, upd, g_tot = {}, {}, {}, {}
    for c in chunks:
        r = rows[c]
        last = c * CHUNK if rev else (c + 1) * CHUNK - 1
        g_tot[c] = g_cum[last:last + 1, :]
        qm[c] = _stack_heads(q_in[r], GLA_HEADS, GLA_DK)
        att[c] = jnp.where(att_mask, _dot_t(qm[c], k_in[r]), 0.0).astype(bf16)
        k_st = (k[r] * jnp.exp(g_tot[c] - g_cum[r])).astype(bf16)
        upd[c] = _tdot(v_bf[r], k_st)
    st = st_ref[...]
    for c in chunks:
        r = rows[c]
        q_st = _dot_t(qm[c], st.astype(bf16))
        for h in range(GLA_HEADS):
            hr = slice(h * CHUNK, (h + 1) * CHUNK)
            hv = slice(h * GLA_DV, (h + 1) * GLA_DV)
            o_h = _dot(att[c][hr], v_bf[r, hv]) + q_st[hr]
            if final:
                o_buf[r, hv] = o_h
            else:
                out_ref[0, r, hv] = o_h
        st = st * jnp.exp(g_tot[c])
        for h in range(GLA_HEADS):
            sel = (lane >= h * GLA_DK) & (lane < (h + 1) * GLA_DK)
            st = st + jnp.where(sel, upd[c][h * GLA_DV:(h + 1) * GLA_DV], 0.0)
    st_ref[...] = st

    if final:
        for r0 in range(0, tb, FIN_ROWS):
            rr = slice(r0, r0 + FIN_ROWS)
            tot = o_buf[rr, :] + obwd_ref[0, rr, :]
            gate = x_ref[0, rr, 2 * GLA_QK + GLA_V:].astype(f32)
            out_ref[0, rr, :] = (_rmsnorm_heads(tot, GLA_HEADS, GLA_DV) * nw_ref[...]
                                 * _silu(gate)).astype(out_ref.dtype)


def _mixer_index_maps(nb, rev):
    pos = (lambda i: nb - 1 - i) if rev else (lambda i: i)
    cur = lambda b, i: (b, pos(i), 0)
    const2 = lambda b, i: (0, 0)
    return cur, const2


def _mixer_call(body, name, x3, small3, params, extra_final, out_w, scratch, rev, final):
    bsz, s, in_w = x3.shape
    tb = MIX_TB[name]
    nb = s // tb
    cur, const2 = _mixer_index_maps(nb, rev)
    in_specs = [pl.BlockSpec((1, tb, in_w), cur), pl.BlockSpec((1, tb, SMALL_W), cur)]
    in_specs += [pl.BlockSpec(p.shape, const2) for p in params]
    args = [x3, small3, *params]
    scratch = list(scratch)
    if final:
        o_other, consts = extra_final
        in_specs += [pl.BlockSpec((1, tb, out_w), cur)] + [pl.BlockSpec(p.shape, const2) for p in consts]
        args += [o_other, *consts]
        scratch += [pltpu.VMEM((tb, out_w), f32)]
    return pl.pallas_call(
        functools.partial(body, rev, final, tb),
        grid=(bsz, nb),
        in_specs=in_specs,
        out_specs=pl.BlockSpec((1, tb, out_w), cur),
        out_shape=jax.ShapeDtypeStruct((bsz, s, out_w), bf16 if final else f32),
        scratch_shapes=scratch,
        compiler_params=pltpu.CompilerParams(dimension_semantics=("arbitrary", "arbitrary"),
                                             vmem_limit_bytes=VMEM_LIMIT),
        name=name + ("_fwd" if final else "_bwd"),
    )(*args)


def _gdn_body(rev, final, tb, *refs):
    (x_ref, small_ref, alog_ref, dtb_ref, e64b_ref, e64g_ref, tri_ref) = refs[:7]
    if final:
        obwd_ref, nw_ref, out_ref, st_ref, o_buf = refs[7:]
    else:
        out_ref, st_ref = refs[7:]
        o_buf = None
    nheads, dh = GDN_HEADS, GDN_DH
    chunks = _chunk_order(tb // CHUNK, rev)
    rows = {c: slice(c * CHUNK, (c + 1) * CHUNK) for c in chunks}
    heads = [slice(h * dh, (h + 1) * dh) for h in range(nheads)]
    hrows = [slice(h * CHUNK, (h + 1) * CHUNK) for h in range(nheads)]

    @pl.when(pl.program_id(1) == 0)
    def _():
        st_ref[...] = jnp.zeros_like(st_ref)

    q = _l2norm_heads(x_ref[0, :, 0:GDN_W].astype(f32), nheads, dh, dh ** -0.5)
    k = _l2norm_heads(x_ref[0, :, GDN_W:2 * GDN_W].astype(f32), nheads, dh, 1.0)
    v = x_ref[0, :, 2 * GDN_W:3 * GDN_W].astype(f32)
    k_bf = k.astype(bf16)
    q_bf = q.astype(bf16)

    sm = small_ref[0]
    beta = jax.nn.sigmoid(sm)
    g_all = -jnp.exp(alog_ref[...]) * jax.nn.softplus(sm + dtb_ref[...])
    g_cum = _chunk_cumsum(tri_ref, g_all)
    gc64 = _expand(_split(g_cum, 2), e64g_ref[...])
    b64 = _expand(_split(beta, 1), e64b_ref[...])
    g_cum_t = g_cum.T
    g_row0 = A_OFF + (nheads if rev else 0)
    b_col0 = BETA_OFF + (nheads if rev else 0)
    eg = jnp.exp(g_cum)
    g_col = [g_cum[:, g_row0 + h:g_row0 + h + 1] for h in range(nheads)]
    eg_col = [eg[:, g_row0 + h:g_row0 + h + 1] for h in range(nheads)]
    b_col = [beta[:, b_col0 + h:b_col0 + h + 1] for h in range(nheads)]
    rhs_k = jnp.concatenate([k[:, heads[h]] * (b_col[h] * eg_col[h]) for h in range(nheads)], axis=1).astype(bf16)
    rhs_v = jnp.concatenate([v[:, heads[h]] * b_col[h] for h in range(nheads)], axis=1).astype(bf16)
    q_dec = jnp.concatenate([q[:, heads[h]] * eg_col[h] for h in range(nheads)], axis=1).astype(bf16)

    incl, strict, ident = _chunk_masks(nheads * CHUNK, rev)
    eye = jnp.where(ident, 1.0, 0.0)

    qk, kk = {}, {}
    for c in chunks:
        qk_l, kk_l = [], []
        for hs in heads:
            res = _dot_t(jnp.concatenate([q_bf[rows[c], hs], k_bf[rows[c], hs]], axis=0), k_bf[rows[c], hs])
            qk_l.append(res[0:CHUNK])
            kk_l.append(res[CHUNK:])
        qk[c] = jnp.concatenate(qk_l, axis=1)
        kk[c] = jnp.concatenate(kk_l, axis=1)
    a_qk, a_neg = {}, {}
    for c in chunks:
        g_row = _row_form(g_cum_t, g_row0, nheads, rows[c])
        decay = jnp.exp(jnp.where(incl, gc64[rows[c]] - g_row, -jnp.inf))
        a_qk[c] = _stack_heads((qk[c] * decay).astype(bf16), nheads, CHUNK)
        a_neg[c] = jnp.where(strict, -(b64[rows[c]] * kk[c] * decay), 0.0)
    t_inv = _unit_triangular_inverse(a_neg, chunks, eye, nheads)
    sol = {}
    for c in chunks:
        tm = _stack_heads(t_inv[c].astype(bf16), nheads, CHUNK)
        r_all = jnp.concatenate(
            [jnp.concatenate([rhs_k[rows[c], hs], rhs_v[rows[c], hs]], axis=1) for hs in heads], axis=0)
        sol[c] = _dot(tm, r_all)

    s = [st_ref[h] for h in range(nheads)]
    for c in chunks:
        r = rows[c]
        last = c * CHUNK if rev else (c + 1) * CHUNK - 1
        g_tot = [g_col[h][last:last + 1, :] for h in range(nheads)]
        dec_u = [jnp.exp(g_tot[h] - g_col[h][r]) for h in range(nheads)]
        res = [_dot(jnp.concatenate([sol[c][hrows[h], 0:dh].astype(bf16), q_dec[r, heads[h]]], axis=0),
                    s[h].astype(bf16)) for h in range(nheads)]
        u = [sol[c][hrows[h], dh:] - res[h][0:CHUNK] for h in range(nheads)]
        upd = [_tdot(k_bf[r, heads[h]], (u[h] * dec_u[h]).astype(bf16)) for h in range(nheads)]
        s = [jnp.exp(g_tot[h]) * s[h] + upd[h] for h in range(nheads)]
        o_intra = _dot(a_qk[c], jnp.concatenate([x.astype(bf16) for x in u], axis=0))
        for h in range(nheads):
            o_h = o_intra[hrows[h]] + res[h][CHUNK:]
            if final:
                o_buf[r, heads[h]] = o_h
            else:
                out_ref[0, r, heads[h]] = o_h
    for h in range(nheads):
        st_ref[h] = s[h]

    if final:
        for r0 in range(0, tb, FIN_ROWS):
            rr = slice(r0, r0 + FIN_ROWS)
            tot = o_buf[rr, :] + obwd_ref[0, rr, :]
            gate = x_ref[0, rr, 3 * GDN_W:].astype(f32)
            out_ref[0, rr, :] = (_rmsnorm_heads(tot, nheads, dh) * nw_ref[...]
                                 * _silu(gate)).astype(out_ref.dtype)


def _ssd_body(rev, final, tb, *refs):
    (x_ref, small_ref, alog_ref, dtb_ref, e_ref, tri_ref) = refs[:6]
    if final:
        ybwd_ref, d_ref, nw_ref, out_ref, st_ref, y_buf = refs[6:]
    else:
        out_ref, st_ref = refs[6:]
        y_buf = None
    chunks = _chunk_order(tb // CHUNK, rev)
    rows = {c: slice(c * CHUNK, (c + 1) * CHUNK) for c in chunks}
    half = SSD_DINNER // SSD_GROUPS
    pair_w = 2 * SSD_HEADDIM
    pairs_per_group = SSD_HPG // 2

    @pl.when(pl.program_id(1) == 0)
    def _():
        st_ref[...] = jnp.zeros_like(st_ref)

    bm = x_ref[0, :, 2 * SSD_DINNER:2 * SSD_DINNER + SSD_BC].astype(bf16)
    cm = x_ref[0, :, 2 * SSD_DINNER + SSD_BC:].astype(bf16)

    dt_all = jax.nn.softplus(small_ref[0] + dtb_ref[...])
    la_all = dt_all * (-jnp.exp(alog_ref[...]))
    e_mat = e_ref[...]
    dt_e = _expand(_split(dt_all, 1), e_mat)
    a_cum = _chunk_cumsum(tri_ref, la_all)
    a_col = _expand(_split(a_cum, 2), e_mat)
    a_cum_t = a_cum.T
    a_row0 = DT_OFF + (SSD_HEADS if rev else 0)
    incl, _, _ = _chunk_masks(pair_w, rev)
    lane = _iota((CHUNK, pair_w), 1)
    zero = jnp.zeros((CHUNK, pair_w), bf16)

    y_intra, d_state, e_tot = {}, {}, {}
    for c in chunks:
        r = rows[c]
        last = c * CHUNK if rev else (c + 1) * CHUNK - 1
        a_tot = a_col[last:last + 1, :]
        a_row = _row_form(a_cum_t, a_row0, SSD_HEADS, r)
        cb2 = []
        for g in range(SSD_GROUPS):
            gs = slice(g * SSD_DSTATE, (g + 1) * SSD_DSTATE)
            cb2.append(_dot_t(cm[r, gs], jnp.concatenate([bm[r, gs], bm[r, gs]], axis=0)))
        xw = []
        for pr in range(SSD_HEADS // 2):
            ps = slice(pr * pair_w, (pr + 1) * pair_w)
            a_cp = a_col[r, ps]
            xp = x_ref[0, r, SSD_DINNER + pr * pair_w:SSD_DINNER + (pr + 1) * pair_w].astype(f32) * dt_e[r, ps]
            xp_bf = xp.astype(bf16)
            decay = jnp.exp(jnp.where(incl, a_cp - a_row[:, ps], -jnp.inf))
            w_att = (decay * cb2[pr // pairs_per_group]).astype(bf16)
            bd = jnp.concatenate([jnp.where(lane < SSD_HEADDIM, xp_bf, zero),
                                  jnp.where(lane >= SSD_HEADDIM, xp_bf, zero)], axis=0)
            y_intra[c, pr] = _dot(w_att, bd)
            xw.append((xp * jnp.exp(a_tot[:, ps] - a_cp)).astype(bf16))
        e_tot[c] = jnp.exp(a_tot)
        d_state[c] = [_tdot(bm[r, g * SSD_DSTATE:(g + 1) * SSD_DSTATE],
                            jnp.concatenate(xw[g * pairs_per_group:(g + 1) * pairs_per_group], axis=1))
                      for g in range(SSD_GROUPS)]

    s = [st_ref[g] for g in range(SSD_GROUPS)]
    for c in chunks:
        r = rows[c]
        for g in range(SSD_GROUPS):
            gl = slice(g * half, (g + 1) * half)
            y_state = _dot(cm[r, g * SSD_DSTATE:(g + 1) * SSD_DSTATE], s[g].astype(bf16))
            s[g] = e_tot[c][:, gl] * s[g] + d_state[c][g]
            for pp in range(pairs_per_group):
                pr = g * pairs_per_group + pp
                ps = slice(pr * pair_w, (pr + 1) * pair_w)
                y_p = y_intra[c, pr] + y_state[:, pp * pair_w:(pp + 1) * pair_w] * jnp.exp(a_col[r, ps])
                if final:
                    y_buf[r, ps] = y_p
                else:
                    out_ref[0, r, ps] = y_p
    for g in range(SSD_GROUPS):
        st_ref[g] = s[g]

    if final:
        for r0 in range(0, tb, FIN_ROWS // 2):
            rr = slice(r0, r0 + FIN_ROWS // 2)
            y = (y_buf[rr, :] + ybwd_ref[0, rr, :]
                 + d_ref[...] * x_ref[0, rr, SSD_DINNER:2 * SSD_DINNER].astype(f32))
            yz = y * _silu(x_ref[0, rr, 0:SSD_DINNER].astype(f32))
            out_ref[0, rr, :] = (_rmsnorm_heads(yz, SSD_GROUPS, half) * nw_ref[...]).astype(out_ref.dtype)


def _outmlp_body(last_layer, *refs):
    if last_layer:
        x_ref, og_ref, od_ref, os_ref, wo_ref, nw_ref, wu_ref, wd_ref, nf_ref, out_ref = refs
    else:
        x_ref, og_ref, od_ref, os_ref, wo_ref, nw_ref, wu_ref, wd_ref, out_ref = refs
    mix = (_dot(og_ref[...], wo_ref[0:GLA_V, :])
           + _dot(od_ref[...], wo_ref[GLA_V:GLA_V + GDN_W, :])
           + _dot(os_ref[...], wo_ref[GLA_V + GDN_W:, :]))
    x1 = x_ref[...] + mix
    h = (x1 * lax.rsqrt(jnp.mean(x1 * x1, axis=-1, keepdims=True) + EPS) * nw_ref[...]).astype(bf16)
    mlp = None
    ff_blk = D_FF // 4
    for j in range(4):
        up = jnp.maximum(_dot(h, wu_ref[:, j * ff_blk:(j + 1) * ff_blk]), 0.0)
        t = _dot((up * up).astype(bf16), wd_ref[j * ff_blk:(j + 1) * ff_blk, :])
        mlp = t if mlp is None else mlp + t
    acc = x1 + mlp
    if last_layer:
        acc = acc * lax.rsqrt(jnp.mean(acc * acc, axis=-1, keepdims=True) + EPS) * nf_ref[...]
    out_ref[...] = acc


def _outmlp(x2d, o_gla, o_gdn, o_ssd, w_out, norm_w, w_up, w_down, norm_f):
    t = x2d.shape[0]
    tm = PROJ_TM
    last_layer = norm_f is not None
    row = lambda i: (i, 0)
    const = lambda i: (0, 0)
    single = pl.Buffered(1)
    in_specs = [pl.BlockSpec((tm, D_MODEL), row),
                pl.BlockSpec((tm, GLA_V), row),
                pl.BlockSpec((tm, GDN_W), row),
                pl.BlockSpec((tm, SSD_DINNER), row),
                pl.BlockSpec((D_MIX, D_MODEL), const, pipeline_mode=single),
                pl.BlockSpec((1, D_MODEL), const),
                pl.BlockSpec((D_MODEL, D_FF), const, pipeline_mode=single),
                pl.BlockSpec((D_FF, D_MODEL), const, pipeline_mode=single)]
    args = [x2d, o_gla, o_gdn, o_ssd, w_out, norm_w.reshape(1, D_MODEL), w_up, w_down]
    if last_layer:
        in_specs.append(pl.BlockSpec((1, D_MODEL), const))
        args.append(norm_f.reshape(1, D_MODEL))
    return pl.pallas_call(
        functools.partial(_outmlp_body, last_layer),
        grid=(t // tm,),
        in_specs=in_specs,
        out_specs=pl.BlockSpec((tm, D_MODEL), row),
        out_shape=jax.ShapeDtypeStruct((t, D_MODEL), f32),
        compiler_params=pltpu.CompilerParams(dimension_semantics=("arbitrary",), vmem_limit_bytes=VMEM_LIMIT),
        name="outmlp_final" if last_layer else "outmlp",
    )(*args)


def _prep_w_in(w):
    s = np.concatenate([[0], np.cumsum(IN_SPLITS)])
    seg = lambda n: w[:, int(s[n]):int(s[n + 1])]
    pad = jnp.zeros((w.shape[0], SMALL_W - 2 * GLA_RANK - 4 * GDN_HEADS - 2 * SSD_HEADS), w.dtype)
    cols = [seg(0), seg(1), seg(2), seg(3),
            seg(5), seg(6),
            seg(9), seg(10),
            seg(4), seg(7), seg(8), seg(11), pad]
    return jnp.concatenate(cols, axis=1).astype(bf16)


def _place_row(vals, off):
    return jnp.zeros((1, SMALL_W), f32).at[0, off:off + vals.shape[0]].set(vals.astype(f32))


def _expand_matrix(off, nheads, width):
    m = np.zeros((SMALL_W, nheads * width), np.float32)
    for h in range(nheads):
        m[off + h, h * width:(h + 1) * width] = 1.0
    return jnp.asarray(m, bf16)


def _cumsum_matrix(tb, rev):
    r = np.arange(tb)
    same = (r[:, None] // CHUNK) == (r[None, :] // CHUNK)
    tri = same & ((r[None, :] >= r[:, None]) if rev else (r[None, :] <= r[:, None]))
    return jnp.asarray(tri, bf16)


def _layer(x3, p):
    bsz, s, _ = x3.shape
    assert all(s % tb == 0 for tb in MIX_TB.values()) and s % PROJ_TM == 0, (bsz, s)
    x2d = x3.reshape(bsz * s, D_MODEL)
    gla_in, gdn_in, ssd_in, small = _inproj(
        x2d, s, p["norm_mix_w"], p["w_in_r"], p["gdn_conv_w"].astype(f32), p["ssd_conv_w"].astype(f32),
        p["ssd_conv_b"].astype(f32).reshape(1, SSD_CONV_CH))
    gla_in = gla_in.reshape(bsz, s, GLA_W)
    gdn_in = gdn_in.reshape(bsz, s, GDN_IN_W)
    ssd_in = ssd_in.reshape(bsz, s, SSD_IN_W)
    small = small.reshape(bsz, s, SMALL_W)

    outs = {}
    for name in ("gla", "gdn", "ssd"):
        prev = None
        for rev in (True, False):
            d = 1 if rev else 0
            final = not rev
            tri = _cumsum_matrix(CUM_ROWS, rev)
            if name == "gla":
                upw = jnp.zeros((SMALL_W, GLA_QK), f32).at[LR_OFF + d * GLA_RANK:LR_OFF + (d + 1) * GLA_RANK].set(
                    p["gla_gk_up"][d].astype(f32)).astype(bf16)
                upb = p["gla_gk_bias"][d].astype(f32).reshape(1, GLA_QK)
                nw = jnp.tile(p["gla_norm_w"].astype(f32), GLA_HEADS).reshape(1, GLA_V)
                prev = _mixer_call(_gla_body, "gla", gla_in, small, (upw, upb, tri), (prev, (nw,)), GLA_V,
                                   [pltpu.VMEM((GLA_DV, GLA_QK), f32)], rev, final)
            elif name == "gdn":
                params = (_place_row(p["gdn_A_log"][d], A_OFF + d * GDN_HEADS),
                          _place_row(p["gdn_dt_bias"][d], A_OFF + d * GDN_HEADS),
                          _expand_matrix(BETA_OFF + d * GDN_HEADS, GDN_HEADS, CHUNK),
                          _expand_matrix(A_OFF + d * GDN_HEADS, GDN_HEADS, CHUNK),
                          tri)
                nw = jnp.tile(p["gdn_norm_w"].astype(f32), GDN_HEADS).reshape(1, GDN_W)
                prev = _mixer_call(_gdn_body, "gdn", gdn_in, small, params, (prev, (nw,)), GDN_W,
                                   [pltpu.VMEM((GDN_HEADS, GDN_DH, GDN_DH), f32)], rev, final)
            else:
                params = (_place_row(p["ssd_A_log"][d], DT_OFF + d * SSD_HEADS),
                          _place_row(p["ssd_dt_bias"][d], DT_OFF + d * SSD_HEADS),
                          _expand_matrix(DT_OFF + d * SSD_HEADS, SSD_HEADS, SSD_HEADDIM),
                          tri)
                d_row = jnp.repeat(p["ssd_D"].astype(f32), SSD_HEADDIM).reshape(1, SSD_DINNER)
                nw = p["ssd_norm_w"].astype(f32).reshape(1, SSD_DINNER)
                prev = _mixer_call(_ssd_body, "ssd", ssd_in, small, params, (prev, (d_row, nw)), SSD_DINNER,
                                   [pltpu.VMEM((SSD_GROUPS, SSD_DSTATE, SSD_HPG * SSD_HEADDIM), f32)], rev, final)
        outs[name] = prev.reshape(bsz * s, -1)

    y = _outmlp(x2d, outs["gla"], outs["gdn"], outs["ssd"], p["w_out"].astype(bf16), p["norm_mlp_w"],
                p["w_up"].astype(bf16), p["w_down"].astype(bf16), p.get("norm_f_w"))
    return y.reshape(bsz, s, D_MODEL)


def kernel(x_prompt, x_sample, norm_mix_w, w_in, gla_gk_up, gla_gk_bias, gla_norm_w, gdn_conv_w, gdn_A_log, gdn_dt_bias, gdn_norm_w, ssd_conv_w, ssd_conv_b, ssd_A_log, ssd_dt_bias, ssd_D, ssd_norm_w, w_out, norm_mlp_w, w_up, w_down, norm_f_w):
    stacked = dict(norm_mix_w=norm_mix_w, gla_gk_up=gla_gk_up, gla_gk_bias=gla_gk_bias, gla_norm_w=gla_norm_w,
                   gdn_conv_w=gdn_conv_w, gdn_A_log=gdn_A_log, gdn_dt_bias=gdn_dt_bias, gdn_norm_w=gdn_norm_w,
                   ssd_conv_w=ssd_conv_w, ssd_conv_b=ssd_conv_b, ssd_A_log=ssd_A_log, ssd_dt_bias=ssd_dt_bias,
                   ssd_D=ssd_D, ssd_norm_w=ssd_norm_w, w_out=w_out, norm_mlp_w=norm_mlp_w, w_up=w_up,
                   w_down=w_down)
    depth = w_in.shape[0]
    layers = []
    for l in range(depth):
        p = {k: v[l] for k, v in stacked.items()}
        p["w_in_r"] = _prep_w_in(w_in[l])
        if l == depth - 1:
            p["norm_f_w"] = norm_f_w
        layers.append(p)

    def trunk(x):
        for p in layers:
            x = _layer(x, p)
        return x

    return (trunk(x_prompt), trunk(x_sample))
```
